```python
import jax, jax.numpy as jnp
from jax import lax
import numpy as np

D_MODEL = 2048
BATCH = 1
SEQ = 8192
DEPTH = 2
DEC_BATCH = 128
DEC_SEQ = 1
PAST_LEN = 16384
PAGE_SIZE = 128

N_A_LAYERS = DEPTH // 2
N_B_LAYERS = DEPTH - N_A_LAYERS
CONV_WIDTH = 31
CONV_DIM = D_MODEL
N_HEADS = D_MODEL // 128
QK_NOPE_DIM = 128
QK_ROPE_DIM = 64
V_HEAD_DIM = 128
Q_LORA_RANK = D_MODEL // 4
KV_LORA_RANK = D_MODEL // 4
LATENT_DIM = KV_LORA_RANK + QK_ROPE_DIM
ROPE_THETA = 10000.0
SOFTMAX_SCALE = (QK_NOPE_DIM + QK_ROPE_DIM) ** -0.5
Q_BLOCK = 128
N_GROUPS = 8
EXPERTS_PER_GROUP = 8
TOP_K_IN_GROUP = 2
EXPERT_FF = D_MODEL // 4
NORM_EPS = 1e-6
LN_EPS = 1e-5

kernel_name = "yoco_conformer_mla_hiermoe_step"

F32 = jnp.float32


def rmsnorm(x, g):
    xf = x.astype(F32)
    y = xf * lax.rsqrt(jnp.mean(xf * xf, axis=-1, keepdims=True) + NORM_EPS)
    return (y * g.astype(F32)).astype(x.dtype)


def layernorm(x, g, b):
    xf = x.astype(F32)
    mu = jnp.mean(xf, axis=-1, keepdims=True)
    var = jnp.mean(jnp.square(xf - mu), axis=-1, keepdims=True)
    y = (xf - mu) * lax.rsqrt(var + LN_EPS)
    return (y * g.astype(F32) + b.astype(F32)).astype(x.dtype)


def rope(x, pos):
    half = x.shape[-1] // 2
    inv = ROPE_THETA ** (-jnp.arange(half, dtype=F32) / half)
    ang = pos.astype(F32)[:, None] * inv[None, :]
    cos = jnp.cos(ang)[None, :, None, :]
    sin = jnp.sin(ang)[None, :, None, :]
    xf = x.astype(F32)
    x1, x2 = xf[..., :half], xf[..., half:]
    return jnp.concatenate([x1 * cos - x2 * sin, x1 * sin + x2 * cos], axis=-1).astype(x.dtype)


def conv_module(xn, prev, w_pw1, b_pw1, w_dw, b_dw, g_ln, b_ln, w_pw2, b_pw2):
    u = xn @ w_pw1 + b_pw1
    glu = u[..., :CONV_DIM] * jax.nn.sigmoid(u[..., CONV_DIM:])
    buf = jnp.concatenate([prev.astype(glu.dtype), glu], axis=1)
    dw = lax.conv_general_dilated(
        buf, w_dw[:, None, :].astype(buf.dtype), window_strides=(1,), padding="VALID",
        dimension_numbers=("NWC", "WIO", "NWC"), feature_group_count=CONV_DIM) + b_dw
    s = jax.nn.silu(layernorm(dw, g_ln, b_ln))
    return s @ w_pw2 + b_pw2, buf[:, -(CONV_WIDTH - 1):]


def hier_moe(x, w_rg, b_rg, w_re, b_re, w_gate, w_up, w_down):
    B, T, D = x.shape
    xt = x.reshape(B * T, D)
    glog = (xt @ w_rg + b_rg).astype(F32)
    gprob = jax.nn.softmax(glog, axis=-1)
    gi = jnp.argmax(glog, axis=-1)
    gp = jnp.take_along_axis(gprob, gi[:, None], axis=-1)[:, 0]
    elog = (xt @ w_re + b_re).astype(F32).reshape(-1, N_GROUPS, EXPERTS_PER_GROUP)
    elog_g = jnp.take_along_axis(elog, gi[:, None, None], axis=1)[:, 0]
    top_val, top_idx = lax.top_k(elog_g, TOP_K_IN_GROUP)
    tw = jax.nn.softmax(top_val, axis=-1) * gp[:, None]
    w_in = jnp.sum(jax.nn.one_hot(top_idx, EXPERTS_PER_GROUP, dtype=F32) * tw[..., None], axis=1)
    combine = jax.nn.one_hot(gi, N_GROUPS, dtype=F32)[:, :, None] * w_in[:, None, :]
    y = jnp.zeros((B * T, D), F32)
    for g in range(N_GROUPS):
        hg = jax.nn.silu(jnp.einsum("nd,edf->nef", xt, w_gate[g])) * jnp.einsum("nd,edf->nef", xt, w_up[g])
        y = y + jnp.einsum("nef,efd->nd", hg * combine[:, g, :, None].astype(hg.dtype), w_down[g])
    return y.astype(x.dtype).reshape(B, T, D)


def shared_kv(h, pos, g_kv_in, w_dkv, g_ckv):
    a = rmsnorm(h, g_kv_in) @ w_dkv
    ckv = rmsnorm(a[..., :KV_LORA_RANK], g_ckv)
    kpe = rope(a[..., None, KV_LORA_RANK:], pos)[:, :, 0]
    return jnp.concatenate([ckv, kpe], axis=-1)


def attend_prompt(q, kv):
    B, T = q.shape[0], q.shape[1]
    nb = T // Q_BLOCK
    qb = q.reshape(B, nb, Q_BLOCK, N_HEADS, LATENT_DIM).transpose(1, 0, 2, 3, 4)
    kvf = kv.astype(F32)
    kpos = jnp.arange(T)

    def block(args):
        i, qi = args
        s = jnp.einsum("bqhc,bkc->bhqk", qi.astype(F32), kvf) * SOFTMAX_SCALE
        qpos = i * Q_BLOCK + jnp.arange(Q_BLOCK)
        s = jnp.where(kpos[None, :] <= qpos[:, None], s, -jnp.inf)
        p = jax.nn.softmax(s, axis=-1)
        return jnp.einsum("bhqk,bkr->bqhr", p, kvf[..., :KV_LORA_RANK])

    out = lax.map(block, (jnp.arange(nb), qb))
    return out.transpose(1, 0, 2, 3, 4).reshape(B, T, N_HEADS, KV_LORA_RANK).astype(q.dtype)


def attend_sample(q, kv_new, cache, page_table):
    Sq = q.shape[1]
    qf = q.astype(F32)
    kvn = kv_new.astype(F32)
    s_new = jnp.einsum("bqhc,bkc->bhqk", qf, kvn) * SOFTMAX_SCALE
    causal = jnp.tril(jnp.ones((Sq, Sq), bool))
    s_new = jnp.where(causal, s_new, -jnp.inf)
    m = jnp.max(s_new, axis=-1)
    p = jnp.exp(s_new - m[..., None])
    l = jnp.sum(p, axis=-1)
    acc = jnp.einsum("bhqk,bkr->bhqr", p, kvn[..., :KV_LORA_RANK])

    def step(carry, pages):
        m, l, acc = carry
        kv = cache[pages].astype(F32)
        s = jnp.einsum("bqhc,bkc->bhqk", qf, kv) * SOFTMAX_SCALE
        m_new = jnp.maximum(m, jnp.max(s, axis=-1))
        corr = jnp.exp(m - m_new)
        ps = jnp.exp(s - m_new[..., None])
        l = l * corr + jnp.sum(ps, axis=-1)
        acc = acc * corr[..., None] + jnp.einsum("bhqk,bkr->bhqr", ps, kv[..., :KV_LORA_RANK])
        return (m_new, l, acc), None

    (m, l, acc), _ = lax.scan(step, (m, l, acc), page_table.T)
    out = acc / l[..., None]
    return out.transpose(0, 2, 1, 3).astype(q.dtype)


def mla_layer(xn, kv_lat, pos, attend, w_dq, g_qa, w_uq, w_uk, w_uv, w_o):
    B, T = xn.shape[0], xn.shape[1]
    cq = rmsnorm(xn @ w_dq, g_qa)
    q = jnp.einsum("btr,rhd->bthd", cq, w_uq)
    q_nope = q[..., :QK_NOPE_DIM]
    q_pe = rope(q[..., QK_NOPE_DIM:], pos)
    q_lat = jnp.einsum("bthn,hrn->bthr", q_nope, w_uk)
    out_lat = attend(jnp.concatenate([q_lat, q_pe], axis=-1), kv_lat)
    o = jnp.einsum("bthr,hrv->bthv", out_lat, w_uv).reshape(B, T, N_HEADS * V_HEAD_DIM)
    return o @ w_o


def run_trunk(x, pos, conv_prev, attend, P):
    h = x
    kv_lat = None
    new_conv = []
    for layer in range(DEPTH):
        xn = rmsnorm(h, P["g_mix"][layer])
        if layer < N_A_LAYERS:
            a = layer
            out, st = conv_module(xn, conv_prev[a], P["w_pw1"][a], P["b_pw1"][a], P["w_dw"][a],
                                  P["b_dw"][a], P["g_conv_ln"][a], P["b_conv_ln"][a],
                                  P["w_pw2"][a], P["b_pw2"][a])
            new_conv.append(st)
        else:
            b = layer - N_A_LAYERS
            out = mla_layer(xn, kv_lat, pos, attend, P["w_dq"][b], P["g_q_a"][b], P["w_uq"][b],
                            P["w_uk"], P["w_uv"], P["w_o"][b])
        h = h + out
        h = h + hier_moe(rmsnorm(h, P["g_ffn"][layer]), P["w_router_group"][layer],
                         P["b_router_group"][layer], P["w_router_expert"][layer],
                         P["b_router_expert"][layer], P["w_exp_gate"][layer],
                         P["w_exp_up"][layer], P["w_exp_down"][layer])
        if layer == N_A_LAYERS - 1:
            kv_lat = shared_kv(h, pos, P["g_kv_in"], P["w_dkv"], P["g_ckv"])
    return rmsnorm(h, P["g_final"]), jnp.stack(new_conv), kv_lat


def setup_inputs(seed: int = 0) -> dict:
    key = jax.random.key(seed)
    ks = iter(jax.random.split(key, 64))

    def nrm(shape, fan_in):
        return jax.random.normal(next(ks), shape, F32) * fan_in ** -0.5

    def gain(shape):
        return 1.0 + 0.02 * jax.random.normal(next(ks), shape, F32)

    def small(shape, s=0.01):
        return s * jax.random.normal(next(ks), shape, F32)

    n_pages = PAST_LEN // PAGE_SIZE
    n_used = DEC_BATCH * n_pages
    n_pool = n_used + n_used // 4
    page_table = jax.random.permutation(next(ks), n_pool)[:n_used].astype(jnp.int32).reshape(DEC_BATCH, n_pages)
    G, E, F = N_GROUPS, EXPERTS_PER_GROUP, EXPERT_FF
    return {
        "x_prompt": jax.random.normal(next(ks), (BATCH, SEQ, D_MODEL), F32),
        "x_sample": jax.random.normal(next(ks), (DEC_BATCH, DEC_SEQ, D_MODEL), F32),
        "state_conv": jax.random.normal(next(ks), (N_A_LAYERS, DEC_BATCH, CONV_WIDTH - 1, CONV_DIM), F32),
        "cache_kv_latent": jax.random.normal(next(ks), (n_pool, PAGE_SIZE, LATENT_DIM), F32),
        "page_table": page_table,
        "g_mix": gain((DEPTH, D_MODEL)),
        "g_ffn": gain((DEPTH, D_MODEL)),
        "g_final": gain((D_MODEL,)),
        "w_pw1": nrm((N_A_LAYERS, D_MODEL, 2 * CONV_DIM), D_MODEL),
        "b_pw1": small((N_A_LAYERS, 2 * CONV_DIM)),
        "w_dw": nrm((N_A_LAYERS, CONV_WIDTH, CONV_DIM), CONV_WIDTH),
        "b_dw": small((N_A_LAYERS, CONV_DIM)),
        "g_conv_ln": gain((N_A_LAYERS, CONV_DIM)),
        "b_conv_ln": small((N_A_LAYERS, CONV_DIM)),
        "w_pw2": nrm((N_A_LAYERS, CONV_DIM, D_MODEL), CONV_DIM),
        "b_pw2": small((N_A_LAYERS, D_MODEL)),
        "g_kv_in": gain((D_MODEL,)),
        "w_dkv": nrm((D_MODEL, LATENT_DIM), D_MODEL),
        "g_ckv": gain((KV_LORA_RANK,)),
        "w_uk": nrm((N_HEADS, KV_LORA_RANK, QK_NOPE_DIM), KV_LORA_RANK),
        "w_uv": nrm((N_HEADS, KV_LORA_RANK, V_HEAD_DIM), KV_LORA_RANK),
        "w_dq": nrm((N_B_LAYERS, D_MODEL, Q_LORA_RANK), D_MODEL),
        "g_q_a": gain((N_B_LAYERS, Q_LORA_RANK)),
        "w_uq": nrm((N_B_LAYERS, Q_LORA_RANK, N_HEADS, QK_NOPE_DIM + QK_ROPE_DIM), Q_LORA_RANK),
        "w_o": nrm((N_B_LAYERS, N_HEADS * V_HEAD_DIM, D_MODEL), N_HEADS * V_HEAD_DIM),
        "w_router_group": nrm((DEPTH, D_MODEL, G), D_MODEL),
        "b_router_group": small((DEPTH, G)),
        "w_router_expert": nrm((DEPTH, D_MODEL, G * E), D_MODEL),
        "b_router_expert": small((DEPTH, G * E)),
        "w_exp_gate": nrm((DEPTH, G, E, D_MODEL, F), D_MODEL),
        "w_exp_up": nrm((DEPTH, G, E, D_MODEL, F), D_MODEL),
        "w_exp_down": nrm((DEPTH, G, E, F, D_MODEL), F),
    }


def reference(x_prompt, x_sample, state_conv, cache_kv_latent, page_table,
              g_mix, g_ffn, g_final, w_pw1, b_pw1, w_dw, b_dw, g_conv_ln, b_conv_ln, w_pw2, b_pw2,
              g_kv_in, w_dkv, g_ckv, w_uk, w_uv, w_dq, g_q_a, w_uq, w_o,
              w_router_group, b_router_group, w_router_expert, b_router_expert,
              w_exp_gate, w_exp_up, w_exp_down):
    P = dict(g_mix=g_mix, g_ffn=g_ffn, g_final=g_final, w_pw1=w_pw1, b_pw1=b_pw1, w_dw=w_dw,
             b_dw=b_dw, g_conv_ln=g_conv_ln, b_conv_ln=b_conv_ln, w_pw2=w_pw2, b_pw2=b_pw2,
             g_kv_in=g_kv_in, w_dkv=w_dkv, g_ckv=g_ckv, w_uk=w_uk, w_uv=w_uv, w_dq=w_dq,
             g_q_a=g_q_a, w_uq=w_uq, w_o=w_o, w_router_group=w_router_group,
             b_router_group=b_router_group, w_router_expert=w_router_expert,
             b_router_expert=b_router_expert, w_exp_gate=w_exp_gate, w_exp_up=w_exp_up,
             w_exp_down=w_exp_down)
    B, T = x_prompt.shape[0], x_prompt.shape[1]
    pos_p = jnp.arange(T, dtype=jnp.int32)
    conv_zero = jnp.zeros((N_A_LAYERS, B, CONV_WIDTH - 1, CONV_DIM), x_prompt.dtype)
    y_prompt, conv_state_prompt, kv_rows_prompt = run_trunk(
        x_prompt, pos_p, conv_zero, attend_prompt, P)
    pos_s = PAST_LEN + jnp.arange(x_sample.shape[1], dtype=jnp.int32)
    attend_s = lambda q, kv: attend_sample(q, kv, cache_kv_latent, page_table)
    y_sample, conv_state_sample, kv_rows_sample = run_trunk(
        x_sample, pos_s, state_conv, attend_s, P)
    return (y_prompt, y_sample, conv_state_prompt, conv_state_sample, kv_rows_prompt, kv_rows_sample)
```

```python
import functools

import jax
import jax.numpy as jnp
import numpy as np
from jax import lax
from jax.experimental import pallas as pl
from jax.experimental.pallas import tpu as pltpu

F32 = jnp.float32
BF16 = jnp.bfloat16

D_MODEL = 2048
N_PROMPT = 8192
N_SAMPLE = 128
N_TOK = N_PROMPT + N_SAMPLE
PAST_LEN = 16384
PAGE_SIZE = 128
N_PAGES = PAST_LEN // PAGE_SIZE
CONV_WIDTH = 31
N_HEADS = 16
QK_NOPE_DIM = 128
QK_ROPE_DIM = 64
V_HEAD_DIM = 128
Q_LORA_RANK = 512
KV_LORA_RANK = 512
LATENT_DIM = KV_LORA_RANK + QK_ROPE_DIM
ROPE_THETA = 10000.0
SOFTMAX_SCALE = (QK_NOPE_DIM + QK_ROPE_DIM) ** -0.5
N_GROUPS = 8
EXPERTS_PER_GROUP = 8
N_EXPERTS = N_GROUPS * EXPERTS_PER_GROUP
EXPERT_FF = 512
NORM_EPS = 1e-6
LN_EPS = 1e-5

LANES = 128
HEAD_PAD = 2 * LANES
KV_EXT = KV_LORA_RANK + LANES

ROW_TILE = 128
MAX_TILES = 2 * N_TOK // ROW_TILE + N_EXPERTS
N_ITEMS = 2 * N_TOK

VMEM_LIMIT = 56 * 1024 * 1024


def _cparams(sem):
    return pltpu.CompilerParams(dimension_semantics=sem, vmem_limit_bytes=VMEM_LIMIT)


def _rms(x, g):
    return x * lax.rsqrt(jnp.mean(x * x, axis=-1, keepdims=True) + NORM_EPS) * g


def _dot(a, b):
    return jnp.dot(a, b, preferred_element_type=F32)


def _dot_nt(a, b):
    return lax.dot_general(a, b, (((1,), (1,)), ((), ())), preferred_element_type=F32)


PW1_TM = 832
PW1_TN = 512


def _pw1_kernel(x_ref, g_ref, wa_ref, wb_ref, ba_ref, bb_ref, o_ref, xn_ref):
    @pl.when(pl.program_id(1) == 0)
    def _():
        xn_ref[...] = _rms(x_ref[...], g_ref[...]).astype(BF16)

    xn = xn_ref[...]
    a = _dot(xn, wa_ref[...]) + ba_ref[...]
    b = _dot(xn, wb_ref[...]) + bb_ref[...]
    o_ref[...] = a * jax.nn.sigmoid(b)


def _pw1_glu(x, g, w, b):
    nj = D_MODEL // PW1_TN
    return pl.pallas_call(
        _pw1_kernel,
        grid=(N_TOK // PW1_TM, nj),
        in_specs=[
            pl.BlockSpec((PW1_TM, D_MODEL), lambda i, j: (i, 0)),
            pl.BlockSpec((1, D_MODEL), lambda i, j: (0, 0)),
            pl.BlockSpec((D_MODEL, PW1_TN), lambda i, j: (0, j)),
            pl.BlockSpec((D_MODEL, PW1_TN), lambda i, j: (0, j + nj)),
            pl.BlockSpec((1, PW1_TN), lambda i, j: (0, j)),
            pl.BlockSpec((1, PW1_TN), lambda i, j: (0, j + nj)),
        ],
        out_specs=pl.BlockSpec((PW1_TM, PW1_TN), lambda i, j: (i, j)),
        out_shape=jax.ShapeDtypeStruct((N_TOK, D_MODEL), F32),
        scratch_shapes=[pltpu.VMEM((PW1_TM, D_MODEL), BF16)],
        compiler_params=_cparams(("parallel", "arbitrary")),
        name="pw1_glu",
    )(x, g, w, w, b, b)


CONV_TM = 256
HALO = 32
CONV_CW = 256
CONV_RB = 64


def _ln_swish_pw2(dw, x, gln, bln, w2, b2):
    mu = jnp.mean(dw, axis=-1, keepdims=True)
    c = dw - mu
    var = jnp.mean(c * c, axis=-1, keepdims=True)
    y = c * lax.rsqrt(var + LN_EPS) * gln + bln
    s = y * jax.nn.sigmoid(y)
    return x + _dot(s.astype(BF16), w2) + b2


def _conv_prompt_kernel(glu_ref, halo_ref, x_ref, wdw_ref, bdw_ref, gln_ref, bln_ref, w2_ref, b2_ref,
                        o_ref, buf_ref, dw_ref):
    first = pl.program_id(0) == 0
    buf_ref[0:HALO, :] = jnp.where(first, 0.0, halo_ref[...])
    buf_ref[HALO:, :] = glu_ref[...]
    off = HALO - (CONV_WIDTH - 1)

    def chunk(c, carry):
        col = pl.multiple_of(c * CONV_CW, CONV_CW)
        for r0 in range(0, CONV_TM, CONV_RB):
            acc = jnp.broadcast_to(bdw_ref[:, pl.ds(col, CONV_CW)], (CONV_RB, CONV_CW))
            for k in range(CONV_WIDTH):
                acc = acc + wdw_ref[k:k + 1, pl.ds(col, CONV_CW)] * buf_ref[off + r0 + k:off + r0 + k + CONV_RB,
                                                                            pl.ds(col, CONV_CW)]
            dw_ref[r0:r0 + CONV_RB, pl.ds(col, CONV_CW)] = acc
        return carry

    lax.fori_loop(0, D_MODEL // CONV_CW, chunk, 0)
    o_ref[...] = _ln_swish_pw2(dw_ref[...], x_ref[...], gln_ref[...], bln_ref[...], w2_ref[...], b2_ref[...])


def _conv_prompt(glu, x, wdw, bdw, gln, bln, w2, b2):
    vec = pl.BlockSpec((1, D_MODEL), lambda i: (0, 0))
    return pl.pallas_call(
        _conv_prompt_kernel,
        grid=(N_PROMPT // CONV_TM,),
        in_specs=[
            pl.BlockSpec((CONV_TM, D_MODEL), lambda i: (i, 0)),
            pl.BlockSpec((HALO, D_MODEL), lambda i: (jnp.maximum(i * (CONV_TM // HALO) - 1, 0), 0)),
            pl.BlockSpec((CONV_TM, D_MODEL), lambda i: (i, 0)),
            pl.BlockSpec((CONV_WIDTH, D_MODEL), lambda i: (0, 0)),
            vec, vec, vec,
            pl.BlockSpec((D_MODEL, D_MODEL), lambda i: (0, 0)),
            vec,
        ],
        out_specs=pl.BlockSpec((CONV_TM, D_MODEL), lambda i: (i, 0)),
        out_shape=jax.ShapeDtypeStruct((N_TOK, D_MODEL), F32),
        scratch_shapes=[pltpu.VMEM((CONV_TM + HALO, D_MODEL), F32), pltpu.VMEM((CONV_TM, D_MODEL), F32)],
        compiler_params=_cparams(("arbitrary",)),
        name="conv_prompt",
    )(glu, glu, x, wdw, bdw, gln, bln, w2, b2)


CONV_SB = 16


def _conv_sample_kernel(h_any, st_ref, glu_ref, x_ref, wdw_ref, bdw_ref, gln_ref, bln_ref, w2_ref, b2_ref, o_ref):
    del h_any
    w_hist = wdw_ref[0:CONV_WIDTH - 1, :]
    rows = [jnp.sum(st_ref[b] * w_hist, axis=0, keepdims=True) for b in range(CONV_SB)]
    dw = jnp.concatenate(rows, axis=0) + wdw_ref[CONV_WIDTH - 1:CONV_WIDTH, :] * glu_ref[...] + bdw_ref[...]
    o_ref[...] = _ln_swish_pw2(dw, x_ref[...], gln_ref[...], bln_ref[...], w2_ref[...], b2_ref[...])


def _conv_sample(h_buf, state, glu, x, wdw, bdw, gln, bln, w2, b2):
    vec = pl.BlockSpec((1, D_MODEL), lambda i: (0, 0))
    base = N_PROMPT // CONV_SB
    rows = pl.BlockSpec((CONV_SB, D_MODEL), lambda i: (base + i, 0))
    return pl.pallas_call(
        _conv_sample_kernel,
        grid=(N_SAMPLE // CONV_SB,),
        in_specs=[
            pl.BlockSpec(memory_space=pl.ANY),
            pl.BlockSpec((CONV_SB, CONV_WIDTH - 1, D_MODEL), lambda i: (i, 0, 0)),
            rows, rows,
            pl.BlockSpec((CONV_WIDTH, D_MODEL), lambda i: (0, 0)),
            vec, vec, vec,
            pl.BlockSpec((D_MODEL, D_MODEL), lambda i: (0, 0)),
            vec,
        ],
        out_specs=rows,
        out_shape=jax.ShapeDtypeStruct((N_TOK, D_MODEL), F32),
        input_output_aliases={0: 0},
        compiler_params=_cparams(("arbitrary",)),
        name="conv_sample",
    )(h_buf, state, glu, x, wdw, bdw, gln, bln, w2, b2)


ROUTER_TM = 640


def _router_kernel(h_ref, g_ref, wr_ref, br_ref, xn_ref, ids_ref, tw_ref):
    xn = _rms(h_ref[...], g_ref[...])
    xn_ref[...] = xn
    lg = lax.dot_general(wr_ref[...], xn, (((1,), (1,)), ((), ())), precision=lax.Precision.HIGHEST,
                         preferred_element_type=F32) + br_ref[...]
    glog = lg[0:N_GROUPS]
    iota = lax.broadcasted_iota(jnp.int32, (N_GROUPS, ROUTER_TM), 0)
    gmax = jnp.max(glog, axis=0, keepdims=True)
    gi = jnp.min(jnp.where(glog == gmax, iota, N_GROUPS), axis=0, keepdims=True)
    gp = 1.0 / jnp.sum(jnp.exp(glog - gmax), axis=0, keepdims=True)
    eg = lg[N_GROUPS:N_GROUPS + EXPERTS_PER_GROUP]
    for g in range(1, N_GROUPS):
        lo = N_GROUPS + g * EXPERTS_PER_GROUP
        eg = jnp.where(gi == g, lg[lo:lo + EXPERTS_PER_GROUP], eg)
    v1 = jnp.max(eg, axis=0, keepdims=True)
    i1 = jnp.min(jnp.where(eg == v1, iota, EXPERTS_PER_GROUP), axis=0, keepdims=True)
    eg2 = jnp.where(iota == i1, -jnp.inf, eg)
    v2 = jnp.max(eg2, axis=0, keepdims=True)
    i2 = jnp.min(jnp.where(eg2 == v2, iota, EXPERTS_PER_GROUP), axis=0, keepdims=True)
    e2 = jnp.exp(v2 - v1)
    inv = gp / (1.0 + e2)
    ids_ref[...] = jnp.concatenate([gi * EXPERTS_PER_GROUP + i1, gi * EXPERTS_PER_GROUP + i2], axis=0)
    tw_ref[...] = jnp.concatenate([inv, inv * e2], axis=0)


def _router(h, g, wr_t, br):
    n_r = N_GROUPS + N_EXPERTS
    return pl.pallas_call(
        _router_kernel,
        grid=(N_TOK // ROUTER_TM,),
        in_specs=[
            pl.BlockSpec((ROUTER_TM, D_MODEL), lambda i: (i, 0)),
            pl.BlockSpec((1, D_MODEL), lambda i: (0, 0)),
            pl.BlockSpec((n_r, D_MODEL), lambda i: (0, 0)),
            pl.BlockSpec((n_r, 1), lambda i: (0, 0)),
        ],
        out_specs=[
            pl.BlockSpec((ROUTER_TM, D_MODEL), lambda i: (i, 0)),
            pl.BlockSpec((2, ROUTER_TM), lambda i: (0, i)),
            pl.BlockSpec((2, ROUTER_TM), lambda i: (0, i)),
        ],
        out_shape=[
            jax.ShapeDtypeStruct((N_TOK, D_MODEL), F32),
            jax.ShapeDtypeStruct((2, N_TOK), jnp.int32),
            jax.ShapeDtypeStruct((2, N_TOK), F32),
        ],
        compiler_params=_cparams(("parallel",)),
        name="router",
    )(h, g, wr_t, br)


def _dispatch_plan(ids, tw):
    flat_e = ids.reshape(-1)
    order = jnp.argsort(flat_e).astype(jnp.int32)
    counts = jnp.sum(flat_e[None, :] == jnp.arange(N_EXPERTS, dtype=jnp.int32)[:, None], axis=1, dtype=jnp.int32)
    ntile = (counts + ROW_TILE - 1) // ROW_TILE
    tile_end = jnp.cumsum(ntile)
    n_active = tile_end[-1]
    start = jnp.cumsum(counts) - counts
    tile_ids = jnp.arange(MAX_TILES, dtype=jnp.int32)
    last = jnp.maximum(n_active - 1, 0)
    tile_e = jnp.searchsorted(tile_end, jnp.minimum(tile_ids, last), side="right").astype(jnp.int32)
    tile_e = jnp.minimum(tile_e, N_EXPERTS - 1)
    within = tile_ids - (tile_end - ntile)[tile_e]
    lane = jnp.arange(ROW_TILE, dtype=jnp.int32)[None, :]
    rank = within[:, None] * ROW_TILE + lane
    valid = (rank < counts[tile_e][:, None]) & (tile_ids < n_active)[:, None]
    src = order[jnp.clip(start[tile_e][:, None] + rank, 0, N_ITEMS - 1)]
    item = jnp.where(valid, src, N_ITEMS + lane)
    roww = jnp.where(valid, tw.reshape(-1)[src], 0.0)
    return tile_e, item.reshape(-1), roww.reshape(-1, 1), n_active.reshape(1)


def _experts_kernel(layer, te_ref, item_ref, nact_ref, xn_hbm, roww_ref, wg_ref, wu_ref, wd_ref, y_hbm,
                    xbuf, ybuf, gsem, ssem):
    del layer, te_ref
    t = pl.program_id(0)

    @pl.when(t < nact_ref[0])
    def _():
        base = t * ROW_TILE

        def gather_copy(r):
            it = item_ref[base + r]
            tok = it - jnp.where(it >= N_TOK, N_TOK, 0) - jnp.where(it >= N_ITEMS, N_TOK, 0)
            return pltpu.make_async_copy(xn_hbm.at[pl.ds(tok, 1), :], xbuf.at[pl.ds(r, 1), :], gsem)

        def scatter_copy(r):
            return pltpu.make_async_copy(ybuf.at[pl.ds(r, 1), :], y_hbm.at[pl.ds(item_ref[base + r], 1), :], ssem)

        def start_gather(r, c):
            gather_copy(r).start()
            return c

        def wait_gather(r, c):
            gather_copy(r).wait()
            return c

        def start_scatter(r, c):
            scatter_copy(r).start()
            return c

        def wait_scatter(r, c):
            scatter_copy(r).wait()
            return c

        lax.fori_loop(0, ROW_TILE, start_gather, 0, unroll=8)
        lax.fori_loop(0, ROW_TILE, wait_gather, 0, unroll=8)
        x = xbuf[...].astype(BF16)
        gate = _dot(x, wg_ref[...].astype(BF16))
        up = _dot(x, wu_ref[...].astype(BF16))
        hid = gate * jax.nn.sigmoid(gate) * up * roww_ref[...]
        ybuf[...] = _dot(hid.astype(BF16), wd_ref[...].astype(BF16))
        lax.fori_loop(0, ROW_TILE, start_scatter, 0, unroll=8)
        lax.fori_loop(0, ROW_TILE, wait_scatter, 0, unroll=8)


def _experts(layer, tile_e, item, roww, n_active, xn, w_gate, w_up, w_down):
    def w_spec(shape):
        return pl.BlockSpec((None, None, None) + shape,
                            lambda t, te, it, na: (layer, te[t] // EXPERTS_PER_GROUP, te[t] % EXPERTS_PER_GROUP, 0, 0))

    grid_spec = pltpu.PrefetchScalarGridSpec(
        num_scalar_prefetch=3,
        grid=(MAX_TILES,),
        in_specs=[
            pl.BlockSpec(memory_space=pl.ANY),
            pl.BlockSpec((ROW_TILE, 1), lambda t, te, it, na: (t, 0)),
            w_spec((D_MODEL, EXPERT_FF)),
            w_spec((D_MODEL, EXPERT_FF)),
            w_spec((EXPERT_FF, D_MODEL)),
        ],
        out_specs=pl.BlockSpec(memory_space=pl.ANY),
        scratch_shapes=[
            pltpu.VMEM((ROW_TILE, D_MODEL), F32),
            pltpu.VMEM((ROW_TILE, D_MODEL), F32),
            pltpu.SemaphoreType.DMA(()),
            pltpu.SemaphoreType.DMA(()),
        ],
    )
    return pl.pallas_call(
        functools.partial(_experts_kernel, layer),
        grid_spec=grid_spec,
        out_shape=jax.ShapeDtypeStruct((N_ITEMS + ROW_TILE, D_MODEL), F32),
        compiler_params=_cparams(("arbitrary",)),
        name=f"experts{layer}",
    )(tile_e, item, n_active, xn, roww, w_gate, w_up, w_down)


def _moe(layer, h, g_ffn, wr_t, br, w_gate, w_up, w_down):
    xn, ids, tw = _router(h, g_ffn, wr_t, br)
    tile_e, item, roww, n_active = _dispatch_plan(ids, tw)
    return _experts(layer, tile_e, item, roww, n_active, xn, w_gate, w_up, w_down)


POST_TM = 320


def _rope_slab(slab, tab):
    z = slab * tab
    return z + pltpu.roll(z, LANES // 2, 1)


def _post_moe0_kernel(h_ref, ya_ref, yb_ref, gkv_ref, wkv_ref, gc_ref, tab_ref, wk_ref, wv_ref,
                      h2_ref, kv_ref, k_ref, v_ref):
    h2 = h_ref[...] + ya_ref[...] + yb_ref[...]
    h2_ref[...] = h2
    a = _dot(_rms(h2, gkv_ref[...]).astype(BF16), wkv_ref[...])
    ckv = _rms(a[:, :KV_LORA_RANK], gc_ref[...])
    pe = _rope_slab(a[:, KV_LORA_RANK:], tab_ref[...])
    kv_ref[:, :KV_LORA_RANK] = ckv
    kv_ref[:, KV_LORA_RANK:] = pe[:, :QK_ROPE_DIM]
    lane = lax.broadcasted_iota(jnp.int32, pe.shape, 1)
    pe_pad = jnp.where(lane < QK_ROPE_DIM, pe, 0.0).astype(BF16)
    cb = ckv.astype(BF16)
    kn = _dot(cb, wk_ref[...]).astype(BF16)
    v_ref[...] = _dot(cb, wv_ref[...]).astype(BF16)
    for hd in range(N_HEADS):
        k_ref[:, hd * HEAD_PAD:hd * HEAD_PAD + LANES] = kn[:, hd * LANES:(hd + 1) * LANES]
        k_ref[:, hd * HEAD_PAD + LANES:(hd + 1) * HEAD_PAD] = pe_pad


def _post_moe0(h, y, g_kv, w_kv_ext, g_ckv, tab, wk, wv):
    nb = N_TOK // POST_TM
    row = lambda w: pl.BlockSpec((POST_TM, w), lambda i: (i, 0))
    full = lambda a: pl.BlockSpec(a.shape, lambda i: (0, 0))
    return pl.pallas_call(
        _post_moe0_kernel,
        grid=(nb,),
        in_specs=[row(D_MODEL), row(D_MODEL), pl.BlockSpec((POST_TM, D_MODEL), lambda i: (i + nb, 0)),
                  full(g_kv), full(w_kv_ext), full(g_ckv), row(LANES), full(wk), full(wv)],
        out_specs=[row(D_MODEL), row(LATENT_DIM), row(N_HEADS * HEAD_PAD), row(N_HEADS * V_HEAD_DIM)],
        out_shape=[
            jax.ShapeDtypeStruct((N_TOK, D_MODEL), F32),
            jax.ShapeDtypeStruct((N_TOK, LATENT_DIM), F32),
            jax.ShapeDtypeStruct((N_TOK, N_HEADS * HEAD_PAD), BF16),
            jax.ShapeDtypeStruct((N_TOK, N_HEADS * V_HEAD_DIM), BF16),
        ],
        compiler_params=_cparams(("parallel",)),
        name="post_moe0",
    )(h, y, y, g_kv, w_kv_ext, g_ckv, tab, wk, wv)


def _q_proj_kernel(h_ref, g_ref, wdq_ref, gq_ref, wuq_ref, tab_ref, q_ref):
    xn = _rms(h_ref[...], g_ref[...]).astype(BF16)
    cq = _rms(_dot(xn, wdq_ref[...]), gq_ref[...]).astype(BF16)
    q = _dot(cq, wuq_ref[...]) * SOFTMAX_SCALE
    tab = tab_ref[...]
    for hd in range(N_HEADS):
        lo = hd * HEAD_PAD
        q_ref[:, lo:lo + LANES] = q[:, lo:lo + LANES].astype(BF16)
        q_ref[:, lo + LANES:lo + HEAD_PAD] = _rope_slab(q[:, lo + LANES:lo + HEAD_PAD], tab).astype(BF16)


def _q_proj(h, g, wdq, gq, wuq_ext, tab):
    row = lambda w: pl.BlockSpec((POST_TM, w), lambda i: (i, 0))
    full = lambda a: pl.BlockSpec(a.shape, lambda i: (0, 0))
    return pl.pallas_call(
        _q_proj_kernel,
        grid=(N_TOK // POST_TM,),
        in_specs=[row(D_MODEL), full(g), full(wdq), full(gq), full(wuq_ext), row(LANES)],
        out_specs=row(N_HEADS * HEAD_PAD),
        out_shape=jax.ShapeDtypeStruct((N_TOK, N_HEADS * HEAD_PAD), BF16),
        compiler_params=_cparams(("parallel",)),
        name="q_proj",
    )(h, g, wdq, gq, wuq_ext, tab)


ATT_T = 1024
ATT_NB = N_PROMPT // ATT_T
_PAIRS = [(qi, kj) for qi in range(ATT_NB) for kj in range(qi + 1)]
_PAIR_Q = np.array([p[0] for p in _PAIRS], np.int32)
_PAIR_K = np.array([p[1] for p in _PAIRS], np.int32)


def _attn_prompt_kernel(pq_ref, pk_ref, q_ref, k_ref, v_ref, o_ref, m_ref, l_ref, acc_ref):
    p = pl.program_id(1)
    qi = pq_ref[p]
    kj = pk_ref[p]

    @pl.when(kj == 0)
    def _():
        m_ref[...] = jnp.full(m_ref.shape, -jnp.inf, F32)
        l_ref[...] = jnp.zeros(l_ref.shape, F32)
        acc_ref[...] = jnp.zeros(acc_ref.shape, F32)

    s = _dot_nt(q_ref[...], k_ref[...])
    rows = lax.broadcasted_iota(jnp.int32, s.shape, 0)
    cols = lax.broadcasted_iota(jnp.int32, s.shape, 1)
    s = jnp.where((kj < qi) | (cols <= rows), s, -jnp.inf)
    m_old = m_ref[...]
    m_new = jnp.maximum(m_old, jnp.max(s, axis=-1, keepdims=True))
    corr = jnp.exp(m_old - m_new)
    e = jnp.exp(s - m_new)
    l_ref[...] = l_ref[...] * corr + jnp.sum(e, axis=-1, keepdims=True)
    acc_ref[...] = acc_ref[...] * corr + _dot(e.astype(BF16), v_ref[...])
    m_ref[...] = m_new

    @pl.when(kj == qi)
    def _():
        o_ref[...] = (acc_ref[...] / l_ref[...]).astype(BF16)


def _attn_prompt(q, k, v):
    grid_spec = pltpu.PrefetchScalarGridSpec(
        num_scalar_prefetch=2,
        grid=(N_HEADS, len(_PAIRS)),
        in_specs=[
            pl.BlockSpec((ATT_T, HEAD_PAD), lambda h, p, pq, pk: (pq[p], h)),
            pl.BlockSpec((ATT_T, HEAD_PAD), lambda h, p, pq, pk: (pk[p], h)),
            pl.BlockSpec((ATT_T, V_HEAD_DIM), lambda h, p, pq, pk: (pk[p], h)),
        ],
        out_specs=pl.BlockSpec((ATT_T, V_HEAD_DIM), lambda h, p, pq, pk: (pq[p], h)),
        scratch_shapes=[
            pltpu.VMEM((ATT_T, 1), F32),
            pltpu.VMEM((ATT_T, 1), F32),
            pltpu.VMEM((ATT_T, V_HEAD_DIM), F32),
        ],
    )
    return pl.pallas_call(
        _attn_prompt_kernel,
        grid_spec=grid_spec,
        out_shape=jax.ShapeDtypeStruct((N_TOK, N_HEADS * V_HEAD_DIM), BF16),
        compiler_params=_cparams(("parallel", "arbitrary")),
        name="attn_prompt",
    )(jnp.asarray(_PAIR_Q), jnp.asarray(_PAIR_K), q, k, v)


def _q_lat_kernel(q_ref, wuk_ref, o_ref):
    q = q_ref[...]
    o_ref[:, :KV_LORA_RANK] = _dot(q[:, :QK_NOPE_DIM], wuk_ref[...]).astype(BF16)
    o_ref[:, KV_LORA_RANK:] = q[:, QK_NOPE_DIM:]


def _q_lat(q, w_uk_t):
    return pl.pallas_call(
        _q_lat_kernel,
        grid=(N_HEADS,),
        in_specs=[
            pl.BlockSpec((N_SAMPLE, HEAD_PAD), lambda h: (N_PROMPT // N_SAMPLE, h)),
            pl.BlockSpec((None, QK_NOPE_DIM, KV_LORA_RANK), lambda h: (h, 0, 0)),
        ],
        out_specs=pl.BlockSpec((None, N_SAMPLE, KV_EXT), lambda h: (h, 0, 0)),
        out_shape=jax.ShapeDtypeStruct((N_HEADS, N_SAMPLE, KV_EXT), BF16),
        compiler_params=_cparams(("parallel",)),
        name="q_lat",
    )(q, w_uk_t)


DEC_PAGES = 16
DEC_CHUNKS = N_PAGES // DEC_PAGES
DEC_STEPS = N_SAMPLE * DEC_CHUNKS


def _attn_sample_kernel(pt_ref, q_ref, kvn_ref, cache_hbm, o_ref, kvbuf, sems, m_ref, l_ref, acc_ref):
    b = pl.program_id(0)
    c = pl.program_id(1)
    step = b * DEC_CHUNKS + c
    slot = step % 2

    def page_copy(st, sl, i):
        pg = pt_ref[st * DEC_PAGES + i]
        return pltpu.make_async_copy(cache_hbm.at[pg], kvbuf.at[sl, i], sems.at[sl])

    def start_all(st, sl):
        for i in range(DEC_PAGES):
            page_copy(st, sl, i).start()

    @pl.when(step == 0)
    def _():
        start_all(0, 0)

    @pl.when(step + 1 < DEC_STEPS)
    def _():
        start_all(step + 1, 1 - slot)

    for i in range(DEC_PAGES):
        page_copy(step, slot, i).wait()

    q = q_ref[...]
    q_lat = q[:, :KV_LORA_RANK]
    q_pe = q[:, KV_LORA_RANK:KV_LORA_RANK + QK_ROPE_DIM]

    @pl.when(c == 0)
    def _():
        kvn = kvn_ref[...].astype(BF16).astype(F32)
        qf = q[:, :LATENT_DIM].astype(F32)
        m_ref[...] = jnp.sum(qf * kvn, axis=-1, keepdims=True)
        l_ref[...] = jnp.ones(l_ref.shape, F32)
        acc_ref[...] = jnp.broadcast_to(kvn[:, :KV_LORA_RANK], acc_ref.shape)

    kv = kvbuf[slot].reshape(DEC_PAGES * PAGE_SIZE, LATENT_DIM).astype(BF16)
    ckv = kv[:, :KV_LORA_RANK]
    s = _dot_nt(q_lat, ckv) + _dot_nt(q_pe, kv[:, KV_LORA_RANK:])
    m_old = m_ref[...]
    m_new = jnp.maximum(m_old, jnp.max(s, axis=-1, keepdims=True))
    corr = jnp.exp(m_old - m_new)
    e = jnp.exp(s - m_new)
    l_ref[...] = l_ref[...] * corr + jnp.sum(e, axis=-1, keepdims=True)
    acc_ref[...] = acc_ref[...] * corr + _dot(e.astype(BF16), ckv)
    m_ref[...] = m_new

    @pl.when(c == DEC_CHUNKS - 1)
    def _():
        o_ref[...] = acc_ref[...] / l_ref[...]


def _attn_sample(page_table, q_lat, kv_new, cache):
    grid_spec = pltpu.PrefetchScalarGridSpec(
        num_scalar_prefetch=1,
        grid=(N_SAMPLE, DEC_CHUNKS),
        in_specs=[
            pl.BlockSpec((None, N_HEADS, KV_EXT), lambda b, c, pt: (b, 0, 0)),
            pl.BlockSpec((None, 1, LATENT_DIM), lambda b, c, pt: (b, 0, 0)),
            pl.BlockSpec(memory_space=pl.ANY),
        ],
        out_specs=pl.BlockSpec((None, N_HEADS, KV_LORA_RANK), lambda b, c, pt: (b, 0, 0)),
        scratch_shapes=[
            pltpu.VMEM((2, DEC_PAGES, PAGE_SIZE, LATENT_DIM), F32),
            pltpu.SemaphoreType.DMA((2,)),
            pltpu.VMEM((N_HEADS, 1), F32),
            pltpu.VMEM((N_HEADS, 1), F32),
            pltpu.VMEM((N_HEADS, KV_LORA_RANK), F32),
        ],
    )
    return pl.pallas_call(
        _attn_sample_kernel,
        grid_spec=grid_spec,
        out_shape=jax.ShapeDtypeStruct((N_SAMPLE, N_HEADS, KV_LORA_RANK), F32),
        compiler_params=_cparams(("arbitrary", "arbitrary")),
        name="attn_sample",
    )(page_table.reshape(-1), q_lat, kv_new, cache)


def _uv_sample_kernel(o_any, lat_ref, wuv_ref, o_ref):
    del o_any
    o_ref[...] = _dot(lat_ref[...].astype(BF16), wuv_ref[...]).astype(BF16)


def _uv_sample(o_buf, lat_t, w_uv):
    return pl.pallas_call(
        _uv_sample_kernel,
        grid=(N_HEADS,),
        in_specs=[
            pl.BlockSpec(memory_space=pl.ANY),
            pl.BlockSpec((None, N_SAMPLE, KV_LORA_RANK), lambda h: (h, 0, 0)),
            pl.BlockSpec((None, KV_LORA_RANK, V_HEAD_DIM), lambda h: (h, 0, 0)),
        ],
        out_specs=pl.BlockSpec((N_SAMPLE, V_HEAD_DIM), lambda h: (N_PROMPT // N_SAMPLE, h)),
        out_shape=jax.ShapeDtypeStruct((N_TOK, N_HEADS * V_HEAD_DIM), BF16),
        input_output_aliases={0: 0},
        compiler_params=_cparams(("arbitrary",)),
        name="uv_sample",
    )(o_buf, lat_t, w_uv)


def _out_proj_kernel(o_ref, h_ref, w_ref, h3_ref):
    h3_ref[...] = h_ref[...] + _dot(o_ref[...], w_ref[...])


def _out_proj(o, h, w_o):
    row = pl.BlockSpec((POST_TM, D_MODEL), lambda i: (i, 0))
    return pl.pallas_call(
        _out_proj_kernel,
        grid=(N_TOK // POST_TM,),
        in_specs=[row, row, pl.BlockSpec((D_MODEL, D_MODEL), lambda i: (0, 0))],
        out_specs=row,
        out_shape=jax.ShapeDtypeStruct((N_TOK, D_MODEL), F32),
        compiler_params=_cparams(("parallel",)),
        name="out_proj",
    )(o, h, w_o)


def _final_kernel(h_ref, ya_ref, yb_ref, g_ref, o_ref):
    o_ref[...] = _rms(h_ref[...] + ya_ref[...] + yb_ref[...], g_ref[...])


def _final(h, y, g):
    nb = N_TOK // POST_TM
    row = pl.BlockSpec((POST_TM, D_MODEL), lambda i: (i, 0))
    return pl.pallas_call(
        _final_kernel,
        grid=(nb,),
        in_specs=[row, row, pl.BlockSpec((POST_TM, D_MODEL), lambda i: (i + nb, 0)),
                  pl.BlockSpec((1, D_MODEL), lambda i: (0, 0))],
        out_specs=row,
        out_shape=jax.ShapeDtypeStruct((N_TOK, D_MODEL), F32),
        compiler_params=_cparams(("parallel",)),
        name="final",
    )(h, y, y, g)


def _rope_table():
    half = QK_ROPE_DIM // 2
    inv = ROPE_THETA ** (-jnp.arange(half, dtype=F32) / half)
    pos = jnp.concatenate([jnp.arange(N_PROMPT, dtype=jnp.int32),
                           jnp.full((N_SAMPLE,), PAST_LEN, jnp.int32)]).astype(F32)
    ang = pos[:, None] * inv[None, :]
    c, s = jnp.cos(ang), jnp.sin(ang)
    return jnp.concatenate([c, c, -s, s], axis=1)


def _rot_half_cols(w):
    half = QK_ROPE_DIM // 2
    return jnp.concatenate([w[..., half:], w[..., :half]], axis=-1)


def kernel(x_prompt, x_sample, state_conv, cache_kv_latent, page_table, g_mix, g_ffn, g_final, w_pw1, b_pw1, w_dw,
           b_dw, g_conv_ln, b_conv_ln, w_pw2, b_pw2, g_kv_in, w_dkv, g_ckv, w_uk, w_uv, w_dq, g_q_a, w_uq, w_o,
           w_router_group, b_router_group, w_router_expert, b_router_expert, w_exp_gate, w_exp_up, w_exp_down):
    vec = lambda a: a.reshape(1, -1)
    x = jnp.concatenate([x_prompt[0], x_sample[:, 0]], axis=0)
    tab = _rope_table()

    w1 = w_pw1[0].astype(BF16)
    w2 = w_pw2[0].astype(BF16)
    w_kv_ext = jnp.concatenate([w_dkv, _rot_half_cols(w_dkv[:, KV_LORA_RANK:])], axis=1).astype(BF16)
    wk = w_uk.transpose(1, 0, 2).reshape(KV_LORA_RANK, N_HEADS * QK_NOPE_DIM).astype(BF16)
    wv = w_uv.transpose(1, 0, 2).reshape(KV_LORA_RANK, N_HEADS * V_HEAD_DIM).astype(BF16)
    w_uk_t = w_uk.transpose(0, 2, 1).astype(BF16)
    w_uv_b = w_uv.astype(BF16)
    wdq = w_dq[0].astype(BF16)
    wuq = w_uq[0]
    wuq_ext = jnp.concatenate([wuq, _rot_half_cols(wuq[..., QK_NOPE_DIM:])], axis=-1)
    wuq_ext = wuq_ext.reshape(Q_LORA_RANK, N_HEADS * HEAD_PAD).astype(BF16)
    wo = w_o[0].astype(BF16)
    wr_t = [jnp.concatenate([w_router_group[l], w_router_expert[l]], axis=1).T for l in range(2)]
    br = [jnp.concatenate([b_router_group[l], b_router_expert[l]]).reshape(-1, 1) for l in range(2)]

    glu = _pw1_glu(x, vec(g_mix[0]), w1, vec(b_pw1[0]))
    conv_args = (w_dw[0], vec(b_dw[0]), vec(g_conv_ln[0]), vec(b_conv_ln[0]), w2, vec(b_pw2[0]))
    h1 = _conv_prompt(glu, x, *conv_args)
    h1 = _conv_sample(h1, state_conv[0], glu, x, *conv_args)

    y0 = _moe(0, h1, vec(g_ffn[0]), wr_t[0], br[0], w_exp_gate, w_exp_up, w_exp_down)
    h2, kv_rows, k_heads, v_heads = _post_moe0(h1, y0, vec(g_kv_in), w_kv_ext, vec(g_ckv), tab, wk, wv)

    q = _q_proj(h2, vec(g_mix[1]), wdq, vec(g_q_a[0]), wuq_ext, tab)
    o = _attn_prompt(q, k_heads, v_heads)
    q_lat = _q_lat(q, w_uk_t).transpose(1, 0, 2)
    lat = _attn_sample(page_table, q_lat, kv_rows[N_PROMPT:].reshape(N_SAMPLE, 1, LATENT_DIM), cache_kv_latent)
    o = _uv_sample(o, lat.transpose(1, 0, 2), w_uv_b)
    h3 = _out_proj(o, h2, wo)

    y1 = _moe(1, h3, vec(g_ffn[1]), wr_t[1], br[1], w_exp_gate, w_exp_up, w_exp_down)
    out = _final(h3, y1, vec(g_final))

    conv_state_prompt = glu[N_PROMPT - (CONV_WIDTH - 1):N_PROMPT][None, None]
    conv_state_sample = jnp.concatenate([state_conv[0][:, 1:], glu[N_PROMPT:][:, None]], axis=1)[None]
    return (out[:N_PROMPT][None], out[N_PROMPT:][:, None], conv_state_prompt, conv_state_sample,
            kv_rows[:N_PROMPT][None], kv_rows[N_PROMPT:][:, None])
```

```python
import functools

import jax
import jax.numpy as jnp
import numpy as np
from jax import lax
from jax.experimental import pallas as pl
from jax.experimental.pallas import tpu as pltpu

F32 = jnp.float32
BF16 = jnp.bfloat16

D_MODEL = 2048
N_PROMPT = 8192
N_SAMPLE = 128
N_TOK = N_PROMPT + N_SAMPLE
PAST_LEN = 16384
PAGE_SIZE = 128
N_PAGES = PAST_LEN // PAGE_SIZE
CONV_WIDTH = 31
N_HEADS = 16
QK_NOPE_DIM = 128
QK_ROPE_DIM = 64
V_HEAD_DIM = 128
Q_LORA_RANK = 512
KV_LORA_RANK = 512
LATENT_DIM = KV_LORA_RANK + QK_ROPE_DIM
ROPE_THETA = 10000.0
SOFTMAX_SCALE = (QK_NOPE_DIM + QK_ROPE_DIM) ** -0.5
N_GROUPS = 8
EXPERTS_PER_GROUP = 8
N_EXPERTS = N_GROUPS * EXPERTS_PER_GROUP
EXPERT_FF = 512
NORM_EPS = 1e-6
LN_EPS = 1e-5

LANES = 128
HEAD_PAD = 2 * LANES
KV_EXT = KV_LORA_RANK + LANES

ROW_TILE = 128
MAX_TILES = 2 * N_TOK // ROW_TILE + N_EXPERTS
N_ITEMS = 2 * N_TOK
TOK_BITS = 14
TOK_MASK = (1 << TOK_BITS) - 1
LOG2E = 1.4426950408889634

VMEM_LIMIT = 56 * 1024 * 1024


def _cparams(sem):
    return pltpu.CompilerParams(dimension_semantics=sem, vmem_limit_bytes=VMEM_LIMIT)


def _rms(x, g):
    return x * lax.rsqrt(jnp.mean(x * x, axis=-1, keepdims=True) + NORM_EPS) * g


def _dot(a, b):
    return jnp.dot(a, b, preferred_element_type=F32)


def _dot_nt(a, b):
    return lax.dot_general(a, b, (((1,), (1,)), ((), ())), preferred_element_type=F32)


PW1_TM = 832
PW1_TN = 512


def _pw1_kernel(x_ref, g_ref, wa_ref, wb_ref, ba_ref, bb_ref, o_ref, xn_ref):
    @pl.when(pl.program_id(1) == 0)
    def _():
        xn_ref[...] = _rms(x_ref[...], g_ref[...]).astype(BF16)

    xn = xn_ref[...]
    a = _dot(xn, wa_ref[...]) + ba_ref[...]
    b = _dot(xn, wb_ref[...]) + bb_ref[...]
    o_ref[...] = a * jax.nn.sigmoid(b)


def _pw1_glu(x, g, w, b):
    nj = D_MODEL // PW1_TN
    return pl.pallas_call(
        _pw1_kernel,
        grid=(N_TOK // PW1_TM, nj),
        in_specs=[
            pl.BlockSpec((PW1_TM, D_MODEL), lambda i, j: (i, 0)),
            pl.BlockSpec((1, D_MODEL), lambda i, j: (0, 0)),
            pl.BlockSpec((D_MODEL, PW1_TN), lambda i, j: (0, j)),
            pl.BlockSpec((D_MODEL, PW1_TN), lambda i, j: (0, j + nj)),
            pl.BlockSpec((1, PW1_TN), lambda i, j: (0, j)),
            pl.BlockSpec((1, PW1_TN), lambda i, j: (0, j + nj)),
        ],
        out_specs=pl.BlockSpec((PW1_TM, PW1_TN), lambda i, j: (i, j)),
        out_shape=jax.ShapeDtypeStruct((N_TOK, D_MODEL), F32),
        scratch_shapes=[pltpu.VMEM((PW1_TM, D_MODEL), BF16)],
        compiler_params=_cparams(("parallel", "arbitrary")),
        name="pw1_glu",
    )(x, g, w, w, b, b)


CONV_TM = 256
HALO = 32
CONV_CW = 256
CONV_RB = 64


def _ln_swish_pw2(dw, x, gln, bln, w2, b2):
    mu = jnp.mean(dw, axis=-1, keepdims=True)
    c = dw - mu
    var = jnp.mean(c * c, axis=-1, keepdims=True)
    y = c * lax.rsqrt(var + LN_EPS) * gln + bln
    s = y * jax.nn.sigmoid(y)
    return x + _dot(s.astype(BF16), w2) + b2


def _conv_prompt_kernel(glu_ref, halo_ref, x_ref, wdw_ref, bdw_ref, gln_ref, bln_ref, w2_ref, b2_ref,
                        o_ref, buf_ref, dw_ref):
    first = pl.program_id(0) == 0
    buf_ref[0:HALO, :] = jnp.where(first, 0.0, halo_ref[...])
    buf_ref[HALO:, :] = glu_ref[...]
    off = HALO - (CONV_WIDTH - 1)

    def chunk(c, carry):
        col = pl.multiple_of(c * CONV_CW, CONV_CW)
        for r0 in range(0, CONV_TM, CONV_RB):
            acc = jnp.broadcast_to(bdw_ref[:, pl.ds(col, CONV_CW)], (CONV_RB, CONV_CW))
            for k in range(CONV_WIDTH):
                acc = acc + wdw_ref[k:k + 1, pl.ds(col, CONV_CW)] * buf_ref[off + r0 + k:off + r0 + k + CONV_RB,
                                                                            pl.ds(col, CONV_CW)]
            dw_ref[r0:r0 + CONV_RB, pl.ds(col, CONV_CW)] = acc
        return carry

    lax.fori_loop(0, D_MODEL // CONV_CW, chunk, 0)
    o_ref[...] = _ln_swish_pw2(dw_ref[...], x_ref[...], gln_ref[...], bln_ref[...], w2_ref[...], b2_ref[...])


def _conv_prompt(glu, x, wdw, bdw, gln, bln, w2, b2):
    vec = pl.BlockSpec((1, D_MODEL), lambda i: (0, 0))
    return pl.pallas_call(
        _conv_prompt_kernel,
        grid=(N_PROMPT // CONV_TM,),
        in_specs=[
            pl.BlockSpec((CONV_TM, D_MODEL), lambda i: (i, 0)),
            pl.BlockSpec((HALO, D_MODEL), lambda i: (jnp.maximum(i * (CONV_TM // HALO) - 1, 0), 0)),
            pl.BlockSpec((CONV_TM, D_MODEL), lambda i: (i, 0)),
            pl.BlockSpec((CONV_WIDTH, D_MODEL), lambda i: (0, 0)),
            vec, vec, vec,
            pl.BlockSpec((D_MODEL, D_MODEL), lambda i: (0, 0)),
            vec,
        ],
        out_specs=pl.BlockSpec((CONV_TM, D_MODEL), lambda i: (i, 0)),
        out_shape=jax.ShapeDtypeStruct((N_TOK, D_MODEL), F32),
        scratch_shapes=[pltpu.VMEM((CONV_TM + HALO, D_MODEL), F32), pltpu.VMEM((CONV_TM, D_MODEL), F32)],
        compiler_params=_cparams(("arbitrary",)),
        name="conv_prompt",
    )(glu, glu, x, wdw, bdw, gln, bln, w2, b2)


CONV_SCW = 256


def _conv_sample_kernel(h_any, st_ref, glu_ref, x_ref, wdw_ref, bdw_ref, gln_ref, bln_ref, w2_ref, b2_ref,
                        o_ref, cs_ref, dw_ref):
    del h_any
    j = pl.program_id(0)
    hist = CONV_WIDTH - 1
    glu = glu_ref[...]
    acc = bdw_ref[...] + wdw_ref[hist:CONV_WIDTH, :] * glu
    for k in range(hist):
        acc = acc + wdw_ref[k:k + 1, :] * st_ref[k]
    dw_ref[:, pl.ds(pl.multiple_of(j * CONV_SCW, CONV_SCW), CONV_SCW)] = acc
    cs_ref[0:hist - 1] = st_ref[1:hist]
    cs_ref[hist - 1] = glu

    @pl.when(j == D_MODEL // CONV_SCW - 1)
    def _():
        o_ref[...] = _ln_swish_pw2(dw_ref[...], x_ref[...], gln_ref[...], bln_ref[...], w2_ref[...], b2_ref[...])


def _conv_sample(h_buf, state_t, glu, x, wdw, bdw, gln, bln, w2, b2):
    hist = CONV_WIDTH - 1
    vec = pl.BlockSpec((1, D_MODEL), lambda j: (0, 0))
    cvec = pl.BlockSpec((1, CONV_SCW), lambda j: (0, j))
    rows = pl.BlockSpec((N_SAMPLE, D_MODEL), lambda j: (N_PROMPT // N_SAMPLE, 0))
    st_spec = pl.BlockSpec((hist, N_SAMPLE, CONV_SCW), lambda j: (0, 0, j))
    return pl.pallas_call(
        _conv_sample_kernel,
        grid=(D_MODEL // CONV_SCW,),
        in_specs=[
            pl.BlockSpec(memory_space=pl.ANY),
            st_spec,
            pl.BlockSpec((N_SAMPLE, CONV_SCW), lambda j: (N_PROMPT // N_SAMPLE, j)),
            rows,
            pl.BlockSpec((CONV_WIDTH, CONV_SCW), lambda j: (0, j)),
            cvec, vec, vec,
            pl.BlockSpec((D_MODEL, D_MODEL), lambda j: (0, 0)),
            vec,
        ],
        out_specs=[rows, st_spec],
        out_shape=[jax.ShapeDtypeStruct((N_TOK, D_MODEL), F32),
                   jax.ShapeDtypeStruct((hist, N_SAMPLE, D_MODEL), F32)],
        scratch_shapes=[pltpu.VMEM((N_SAMPLE, D_MODEL), F32)],
        input_output_aliases={0: 0},
        compiler_params=_cparams(("arbitrary",)),
        name="conv_sample",
    )(h_buf, state_t, glu, x, wdw, bdw, gln, bln, w2, b2)


ROUTER_TM = 640


def _router_kernel(h_ref, g_ref, wr_ref, br_ref, xn_ref, ids_ref, tw_ref):
    xn = _rms(h_ref[...], g_ref[...])
    xn_ref[...] = xn
    lg = lax.dot_general(wr_ref[...], xn, (((1,), (1,)), ((), ())), precision=lax.Precision.HIGHEST,
                         preferred_element_type=F32) + br_ref[...]
    glog = lg[0:N_GROUPS]
    iota = lax.broadcasted_iota(jnp.int32, (N_GROUPS, ROUTER_TM), 0)
    gmax = jnp.max(glog, axis=0, keepdims=True)
    gi = jnp.min(jnp.where(glog == gmax, iota, N_GROUPS), axis=0, keepdims=True)
    gp = 1.0 / jnp.sum(jnp.exp(glog - gmax), axis=0, keepdims=True)
    eg = lg[N_GROUPS:N_GROUPS + EXPERTS_PER_GROUP]
    for g in range(1, N_GROUPS):
        lo = N_GROUPS + g * EXPERTS_PER_GROUP
        eg = jnp.where(gi == g, lg[lo:lo + EXPERTS_PER_GROUP], eg)
    v1 = jnp.max(eg, axis=0, keepdims=True)
    i1 = jnp.min(jnp.where(eg == v1, iota, EXPERTS_PER_GROUP), axis=0, keepdims=True)
    eg2 = jnp.where(iota == i1, -jnp.inf, eg)
    v2 = jnp.max(eg2, axis=0, keepdims=True)
    i2 = jnp.min(jnp.where(eg2 == v2, iota, EXPERTS_PER_GROUP), axis=0, keepdims=True)
    e2 = jnp.exp(v2 - v1)
    inv = gp / (1.0 + e2)
    ids_ref[...] = jnp.concatenate([gi * EXPERTS_PER_GROUP + i1, gi * EXPERTS_PER_GROUP + i2], axis=0)
    tw_ref[...] = jnp.concatenate([inv, inv * e2], axis=0)


def _router(h, g, wr_t, br):
    n_r = N_GROUPS + N_EXPERTS
    return pl.pallas_call(
        _router_kernel,
        grid=(N_TOK // ROUTER_TM,),
        in_specs=[
            pl.BlockSpec((ROUTER_TM, D_MODEL), lambda i: (i, 0)),
            pl.BlockSpec((1, D_MODEL), lambda i: (0, 0)),
            pl.BlockSpec((n_r, D_MODEL), lambda i: (0, 0)),
            pl.BlockSpec((n_r, 1), lambda i: (0, 0)),
        ],
        out_specs=[
            pl.BlockSpec((ROUTER_TM, D_MODEL), lambda i: (i, 0)),
            pl.BlockSpec((2, ROUTER_TM), lambda i: (0, i)),
            pl.BlockSpec((2, ROUTER_TM), lambda i: (0, i)),
        ],
        out_shape=[
            jax.ShapeDtypeStruct((N_TOK, D_MODEL), F32),
            jax.ShapeDtypeStruct((2, N_TOK), jnp.int32),
            jax.ShapeDtypeStruct((2, N_TOK), F32),
        ],
        compiler_params=_cparams(("parallel",)),
        name="router",
    )(h, g, wr_t, br)


def _dispatch_plan(ids, tw):
    flat_e = ids.reshape(-1)
    order = jnp.argsort(flat_e).astype(jnp.int32)
    counts = jnp.sum(flat_e[None, :] == jnp.arange(N_EXPERTS, dtype=jnp.int32)[:, None], axis=1, dtype=jnp.int32)
    ntile = (counts + ROW_TILE - 1) // ROW_TILE
    tile_end = jnp.cumsum(ntile)
    n_active = tile_end[-1]
    start = jnp.cumsum(counts) - counts
    tile_ids = jnp.arange(MAX_TILES + 1, dtype=jnp.int32)
    last = jnp.maximum(n_active - 1, 0)
    tile_e = jnp.searchsorted(tile_end, jnp.minimum(tile_ids, last), side="right").astype(jnp.int32)
    tile_e = jnp.minimum(tile_e, N_EXPERTS - 1)
    within = tile_ids - (tile_end - ntile)[tile_e]
    lane = jnp.arange(ROW_TILE, dtype=jnp.int32)[None, :]
    rank = within[:, None] * ROW_TILE + lane
    valid = (rank < counts[tile_e][:, None]) & (tile_ids < n_active)[:, None]
    src = order[jnp.clip(start[tile_e][:, None] + rank, 0, N_ITEMS - 1)]
    dump = N_ITEMS + (tile_ids % 2)[:, None] * ROW_TILE + lane
    item = jnp.where(valid, src, dump)
    tok = item - jnp.where(item >= N_TOK, N_TOK, 0) - jnp.where(item >= N_ITEMS, N_TOK, 0)
    roww = jnp.where(valid, tw.reshape(-1)[src], 0.0)
    packed = (item << TOK_BITS) | tok
    return tile_e[:MAX_TILES], packed.reshape(-1), roww[:MAX_TILES].reshape(-1, 1), n_active.reshape(1)


def _experts_kernel(layer, te_ref, item_ref, nact_ref, xn_hbm, roww_ref, wg_ref, wu_ref, wd_ref, y_hbm,
                    xbuf, ybuf, gsem, ssem):
    del layer, te_ref
    t = pl.program_id(0)
    nact = nact_ref[0]
    slot = t % 2

    def start_gather(tile, sl):
        for r in range(ROW_TILE):
            tok = item_ref[tile * ROW_TILE + r] & TOK_MASK
            pltpu.make_async_copy(xn_hbm.at[pl.ds(tok, 1), :], xbuf.at[sl, pl.ds(r, 1), :], gsem.at[sl]).start()

    def start_scatter(tile, sl):
        for r in range(ROW_TILE):
            it = item_ref[tile * ROW_TILE + r] >> TOK_BITS
            pltpu.make_async_copy(ybuf.at[sl, pl.ds(r, 1), :], y_hbm.at[pl.ds(it, 1), :], ssem.at[sl]).start()

    def wait_gather(sl):
        pltpu.make_async_copy(xn_hbm.at[pl.ds(0, ROW_TILE), :], xbuf.at[sl], gsem.at[sl]).wait()

    def wait_scatter(sl):
        pltpu.make_async_copy(ybuf.at[sl], y_hbm.at[pl.ds(0, ROW_TILE), :], ssem.at[sl]).wait()

    @pl.when((t == 0) & (nact > 0))
    def _():
        start_gather(0, 0)

    @pl.when(t < nact)
    def _():
        wait_gather(slot)

        @pl.when(t >= 2)
        def _():
            wait_scatter(slot)

        start_gather(t + 1, 1 - slot)
        x = xbuf[slot].astype(BF16)
        gate = _dot(x, wg_ref[...].astype(BF16))
        up = _dot(x, wu_ref[...].astype(BF16))
        hid = gate * jax.nn.sigmoid(gate) * up * roww_ref[...]
        ybuf[slot] = _dot(hid.astype(BF16), wd_ref[...].astype(BF16))
        start_scatter(t, slot)

        @pl.when(t == nact - 1)
        def _():
            wait_gather(1 - slot)
            wait_scatter(slot)

            @pl.when(t >= 1)
            def _():
                wait_scatter(1 - slot)


def _experts(layer, tile_e, item, roww, n_active, xn, w_gate, w_up, w_down):
    def w_spec(shape):
        return pl.BlockSpec((None, None, None) + shape,
                            lambda t, te, it, na: (layer, te[t] // EXPERTS_PER_GROUP, te[t] % EXPERTS_PER_GROUP, 0, 0))

    grid_spec = pltpu.PrefetchScalarGridSpec(
        num_scalar_prefetch=3,
        grid=(MAX_TILES,),
        in_specs=[
            pl.BlockSpec(memory_space=pl.ANY),
            pl.BlockSpec((ROW_TILE, 1), lambda t, te, it, na: (t, 0)),
            w_spec((D_MODEL, EXPERT_FF)),
            w_spec((D_MODEL, EXPERT_FF)),
            w_spec((EXPERT_FF, D_MODEL)),
        ],
        out_specs=pl.BlockSpec(memory_space=pl.ANY),
        scratch_shapes=[
            pltpu.VMEM((2, ROW_TILE, D_MODEL), F32),
            pltpu.VMEM((2, ROW_TILE, D_MODEL), F32),
            pltpu.SemaphoreType.DMA((2,)),
            pltpu.SemaphoreType.DMA((2,)),
        ],
    )
    return pl.pallas_call(
        functools.partial(_experts_kernel, layer),
        grid_spec=grid_spec,
        out_shape=jax.ShapeDtypeStruct((N_ITEMS + 2 * ROW_TILE, D_MODEL), F32),
        compiler_params=_cparams(("arbitrary",)),
        name=f"experts{layer}",
    )(tile_e, item, n_active, xn, roww, w_gate, w_up, w_down)


def _moe(layer, h, g_ffn, wr_t, br, w_gate, w_up, w_down):
    xn, ids, tw = _router(h, g_ffn, wr_t, br)
    tile_e, item, roww, n_active = _dispatch_plan(ids, tw)
    return _experts(layer, tile_e, item, roww, n_active, xn, w_gate, w_up, w_down)


POST_TM = 320


def _rope_slab(slab, tab):
    z = slab * tab
    return z + pltpu.roll(z, LANES // 2, 1)


def _post_moe0_kernel(h_ref, ya_ref, yb_ref, gkv_ref, wkv_ref, gc_ref, tab_ref, wk_ref, wv_ref,
                      h2_ref, kv_ref, k_ref, v_ref):
    h2 = h_ref[...] + ya_ref[...] + yb_ref[...]
    h2_ref[...] = h2
    a = _dot(_rms(h2, gkv_ref[...]).astype(BF16), wkv_ref[...])
    ckv = _rms(a[:, :KV_LORA_RANK], gc_ref[...])
    pe = _rope_slab(a[:, KV_LORA_RANK:], tab_ref[...])
    kv_ref[:, :KV_LORA_RANK] = ckv
    kv_ref[:, KV_LORA_RANK:] = pe[:, :QK_ROPE_DIM]
    lane = lax.broadcasted_iota(jnp.int32, pe.shape, 1)
    pe_pad = jnp.where(lane < QK_ROPE_DIM, pe, 0.0).astype(BF16)
    cb = ckv.astype(BF16)
    kn = _dot(cb, wk_ref[...]).astype(BF16)
    v_ref[...] = _dot(cb, wv_ref[...]).astype(BF16)
    for hd in range(N_HEADS):
        k_ref[:, hd * HEAD_PAD:hd * HEAD_PAD + LANES] = kn[:, hd * LANES:(hd + 1) * LANES]
        k_ref[:, hd * HEAD_PAD + LANES:(hd + 1) * HEAD_PAD] = pe_pad


def _post_moe0(h, y, g_kv, w_kv_ext, g_ckv, tab, wk, wv):
    nb = N_TOK // POST_TM
    row = lambda w: pl.BlockSpec((POST_TM, w), lambda i: (i, 0))
    full = lambda a: pl.BlockSpec(a.shape, lambda i: (0, 0))
    return pl.pallas_call(
        _post_moe0_kernel,
        grid=(nb,),
        in_specs=[row(D_MODEL), row(D_MODEL), pl.BlockSpec((POST_TM, D_MODEL), lambda i: (i + nb, 0)),
                  full(g_kv), full(w_kv_ext), full(g_ckv), row(LANES), full(wk), full(wv)],
        out_specs=[row(D_MODEL), row(LATENT_DIM), row(N_HEADS * HEAD_PAD), row(N_HEADS * V_HEAD_DIM)],
        out_shape=[
            jax.ShapeDtypeStruct((N_TOK, D_MODEL), F32),
            jax.ShapeDtypeStruct((N_TOK, LATENT_DIM), F32),
            jax.ShapeDtypeStruct((N_TOK, N_HEADS * HEAD_PAD), BF16),
            jax.ShapeDtypeStruct((N_TOK, N_HEADS * V_HEAD_DIM), BF16),
        ],
        compiler_params=_cparams(("parallel",)),
        name="post_moe0",
    )(h, y, y, g_kv, w_kv_ext, g_ckv, tab, wk, wv)


def _q_proj_kernel(h_ref, g_ref, wdq_ref, gq_ref, wuq_ref, tab_ref, q_ref):
    xn = _rms(h_ref[...], g_ref[...]).astype(BF16)
    cq = _rms(_dot(xn, wdq_ref[...]), gq_ref[...]).astype(BF16)
    q = _dot(cq, wuq_ref[...]) * (SOFTMAX_SCALE * LOG2E)
    tab = tab_ref[...]
    for hd in range(N_HEADS):
        lo = hd * HEAD_PAD
        q_ref[:, lo:lo + LANES] = q[:, lo:lo + LANES].astype(BF16)
        q_ref[:, lo + LANES:lo + HEAD_PAD] = _rope_slab(q[:, lo + LANES:lo + HEAD_PAD], tab).astype(BF16)


def _q_proj(h, g, wdq, gq, wuq_ext, tab):
    row = lambda w: pl.BlockSpec((POST_TM, w), lambda i: (i, 0))
    full = lambda a: pl.BlockSpec(a.shape, lambda i: (0, 0))
    return pl.pallas_call(
        _q_proj_kernel,
        grid=(N_TOK // POST_TM,),
        in_specs=[row(D_MODEL), full(g), full(wdq), full(gq), full(wuq_ext), row(LANES)],
        out_specs=row(N_HEADS * HEAD_PAD),
        out_shape=jax.ShapeDtypeStruct((N_TOK, N_HEADS * HEAD_PAD), BF16),
        compiler_params=_cparams(("parallel",)),
        name="q_proj",
    )(h, g, wdq, gq, wuq_ext, tab)


ATT_T = 1024
ATT_NB = N_PROMPT // ATT_T
ATT_RC = 256
_PAIRS = [(qi, kj) for qi in range(ATT_NB) for kj in range(qi + 1)]
_PAIR_Q = np.array([p[0] for p in _PAIRS], np.int32)
_PAIR_K = np.array([p[1] for p in _PAIRS], np.int32)


def _attn_tile(q_ref, k_ref, v_ref, m_ref, acc_ref, diagonal):
    k = k_ref[...]
    v_ext = jnp.concatenate([v_ref[...], jnp.ones((ATT_T, LANES), BF16)], axis=1)
    for r0 in range(0, ATT_T, ATT_RC):
        rs = slice(r0, r0 + ATT_RC)
        s = _dot_nt(q_ref[rs, :], k)
        if diagonal:
            rows = r0 + lax.broadcasted_iota(jnp.int32, s.shape, 0)
            cols = lax.broadcasted_iota(jnp.int32, s.shape, 1)
            s = jnp.where(cols <= rows, s, -jnp.inf)
        m_old = m_ref[rs, :]
        m_new = jnp.maximum(m_old, jnp.max(s, axis=-1, keepdims=True))
        p = jnp.exp2(s - m_new).astype(BF16)
        acc_ref[rs, :] = acc_ref[rs, :] * jnp.exp2(m_old - m_new) + _dot(p, v_ext)
        m_ref[rs, :] = m_new


def _attn_prompt_kernel(pq_ref, pk_ref, q_ref, k_ref, v_ref, o_ref, m_ref, acc_ref):
    p = pl.program_id(1)
    qi = pq_ref[p]
    kj = pk_ref[p]

    @pl.when(kj == 0)
    def _():
        m_ref[...] = jnp.full(m_ref.shape, -jnp.inf, F32)
        acc_ref[...] = jnp.zeros(acc_ref.shape, F32)

    @pl.when(kj < qi)
    def _():
        _attn_tile(q_ref, k_ref, v_ref, m_ref, acc_ref, diagonal=False)

    @pl.when(kj == qi)
    def _():
        _attn_tile(q_ref, k_ref, v_ref, m_ref, acc_ref, diagonal=True)
        acc = acc_ref[...]
        o_ref[...] = (acc[:, :V_HEAD_DIM] / acc[:, V_HEAD_DIM:]).astype(BF16)


def _attn_prompt(q, k, v):
    grid_spec = pltpu.PrefetchScalarGridSpec(
        num_scalar_prefetch=2,
        grid=(N_HEADS, len(_PAIRS)),
        in_specs=[
            pl.BlockSpec((ATT_T, HEAD_PAD), lambda h, p, pq, pk: (pq[p], h)),
            pl.BlockSpec((ATT_T, HEAD_PAD), lambda h, p, pq, pk: (pk[p], h)),
            pl.BlockSpec((ATT_T, V_HEAD_DIM), lambda h, p, pq, pk: (pk[p], h)),
        ],
        out_specs=pl.BlockSpec((ATT_T, V_HEAD_DIM), lambda h, p, pq, pk: (pq[p], h)),
        scratch_shapes=[
            pltpu.VMEM((ATT_T, 1), F32),
            pltpu.VMEM((ATT_T, V_HEAD_DIM + LANES), F32),
        ],
    )
    return pl.pallas_call(
        _attn_prompt_kernel,
        grid_spec=grid_spec,
        out_shape=jax.ShapeDtypeStruct((N_TOK, N_HEADS * V_HEAD_DIM), BF16),
        compiler_params=_cparams(("parallel", "arbitrary")),
        name="attn_prompt",
    )(jnp.asarray(_PAIR_Q), jnp.asarray(_PAIR_K), q, k, v)


def _q_lat_kernel(q_ref, wuk_ref, o_ref):
    q = q_ref[...]
    o_ref[:, :KV_LORA_RANK] = _dot(q[:, :QK_NOPE_DIM], wuk_ref[...]).astype(BF16)
    o_ref[:, KV_LORA_RANK:] = q[:, QK_NOPE_DIM:]


def _q_lat(q, w_uk_t):
    return pl.pallas_call(
        _q_lat_kernel,
        grid=(N_HEADS,),
        in_specs=[
            pl.BlockSpec((N_SAMPLE, HEAD_PAD), lambda h: (N_PROMPT // N_SAMPLE, h)),
            pl.BlockSpec((None, QK_NOPE_DIM, KV_LORA_RANK), lambda h: (h, 0, 0)),
        ],
        out_specs=pl.BlockSpec((None, N_SAMPLE, KV_EXT), lambda h: (h, 0, 0)),
        out_shape=jax.ShapeDtypeStruct((N_HEADS, N_SAMPLE, KV_EXT), BF16),
        compiler_params=_cparams(("parallel",)),
        name="q_lat",
    )(q, w_uk_t)


DEC_PAGES = 16
DEC_CHUNKS = N_PAGES // DEC_PAGES
DEC_STEPS = N_SAMPLE * DEC_CHUNKS


def _attn_sample_kernel(pt_ref, q_ref, kvn_ref, cache_hbm, o_ref, kvbuf, sems, m_ref, l_ref, acc_ref):
    b = pl.program_id(0)
    c = pl.program_id(1)
    step = b * DEC_CHUNKS + c
    slot = step % 2

    def page_copy(st, sl, i):
        pg = pt_ref[st * DEC_PAGES + i]
        return pltpu.make_async_copy(cache_hbm.at[pg], kvbuf.at[sl, i], sems.at[sl])

    def start_all(st, sl):
        for i in range(DEC_PAGES):
            page_copy(st, sl, i).start()

    @pl.when(step == 0)
    def _():
        start_all(0, 0)

    @pl.when(step + 1 < DEC_STEPS)
    def _():
        start_all(step + 1, 1 - slot)

    for i in range(DEC_PAGES):
        page_copy(step, slot, i).wait()

    q = q_ref[...][:, :LATENT_DIM]

    @pl.when(c == 0)
    def _():
        kvn = kvn_ref[...].astype(BF16).astype(F32)
        m_ref[...] = jnp.sum(q.astype(F32) * kvn, axis=-1, keepdims=True)
        l_ref[...] = jnp.ones(l_ref.shape, F32)
        acc_ref[...] = jnp.broadcast_to(kvn[:, :KV_LORA_RANK], acc_ref.shape)

    kv_t = jnp.concatenate([kvbuf[slot, i].astype(BF16) for i in range(DEC_PAGES)], axis=1)
    s = _dot(q, kv_t)
    m_old = m_ref[...]
    m_new = jnp.maximum(m_old, jnp.max(s, axis=-1, keepdims=True))
    corr = jnp.exp2(m_old - m_new)
    e = jnp.exp2(s - m_new)
    l_ref[...] = l_ref[...] * corr + jnp.sum(e, axis=-1, keepdims=True)
    acc_ref[...] = acc_ref[...] * corr + _dot_nt(e.astype(BF16), kv_t[:KV_LORA_RANK])
    m_ref[...] = m_new

    @pl.when(c == DEC_CHUNKS - 1)
    def _():
        o_ref[...] = acc_ref[...] / l_ref[...]


def _attn_sample(page_table, q_lat, kv_new, cache_t):
    grid_spec = pltpu.PrefetchScalarGridSpec(
        num_scalar_prefetch=1,
        grid=(N_SAMPLE, DEC_CHUNKS),
        in_specs=[
            pl.BlockSpec((None, N_HEADS, KV_EXT), lambda b, c, pt: (b, 0, 0)),
            pl.BlockSpec((None, 1, LATENT_DIM), lambda b, c, pt: (b, 0, 0)),
            pl.BlockSpec(memory_space=pl.ANY),
        ],
        out_specs=pl.BlockSpec((None, N_HEADS, KV_LORA_RANK), lambda b, c, pt: (b, 0, 0)),
        scratch_shapes=[
            pltpu.VMEM((2, DEC_PAGES, LATENT_DIM, PAGE_SIZE), F32),
            pltpu.SemaphoreType.DMA((2,)),
            pltpu.VMEM((N_HEADS, 1), F32),
            pltpu.VMEM((N_HEADS, 1), F32),
            pltpu.VMEM((N_HEADS, KV_LORA_RANK), F32),
        ],
    )
    return pl.pallas_call(
        _attn_sample_kernel,
        grid_spec=grid_spec,
        out_shape=jax.ShapeDtypeStruct((N_SAMPLE, N_HEADS, KV_LORA_RANK), F32),
        compiler_params=_cparams(("arbitrary", "arbitrary")),
        name="attn_sample",
    )(page_table.reshape(-1), q_lat, kv_new, cache_t)


def _uv_sample_kernel(o_any, lat_ref, wuv_ref, o_ref):
    del o_any
    o_ref[...] = _dot(lat_ref[...].astype(BF16), wuv_ref[...]).astype(BF16)


def _uv_sample(o_buf, lat_t, w_uv):
    return pl.pallas_call(
        _uv_sample_kernel,
        grid=(N_HEADS,),
        in_specs=[
            pl.BlockSpec(memory_space=pl.ANY),
            pl.BlockSpec((None, N_SAMPLE, KV_LORA_RANK), lambda h: (h, 0, 0)),
            pl.BlockSpec((None, KV_LORA_RANK, V_HEAD_DIM), lambda h: (h, 0, 0)),
        ],
        out_specs=pl.BlockSpec((N_SAMPLE, V_HEAD_DIM), lambda h: (N_PROMPT // N_SAMPLE, h)),
        out_shape=jax.ShapeDtypeStruct((N_TOK, N_HEADS * V_HEAD_DIM), BF16),
        input_output_aliases={0: 0},
        compiler_params=_cparams(("arbitrary",)),
        name="uv_sample",
    )(o_buf, lat_t, w_uv)


def _out_proj_kernel(o_ref, h_ref, w_ref, h3_ref):
    h3_ref[...] = h_ref[...] + _dot(o_ref[...], w_ref[...])


def _out_proj(o, h, w_o):
    row = pl.BlockSpec((POST_TM, D_MODEL), lambda i: (i, 0))
    return pl.pallas_call(
        _out_proj_kernel,
        grid=(N_TOK // POST_TM,),
        in_specs=[row, row, pl.BlockSpec((D_MODEL, D_MODEL), lambda i: (0, 0))],
        out_specs=row,
        out_shape=jax.ShapeDtypeStruct((N_TOK, D_MODEL), F32),
        compiler_params=_cparams(("parallel",)),
        name="out_proj",
    )(o, h, w_o)


def _final_kernel(h_ref, ya_ref, yb_ref, g_ref, o_ref):
    o_ref[...] = _rms(h_ref[...] + ya_ref[...] + yb_ref[...], g_ref[...])


def _final(h, y, g):
    nb = N_TOK // POST_TM
    row = pl.BlockSpec((POST_TM, D_MODEL), lambda i: (i, 0))
    return pl.pallas_call(
        _final_kernel,
        grid=(nb,),
        in_specs=[row, row, pl.BlockSpec((POST_TM, D_MODEL), lambda i: (i + nb, 0)),
                  pl.BlockSpec((1, D_MODEL), lambda i: (0, 0))],
        out_specs=row,
        out_shape=jax.ShapeDtypeStruct((N_TOK, D_MODEL), F32),
        compiler_params=_cparams(("parallel",)),
        name="final",
    )(h, y, y, g)


def _rope_table():
    half = QK_ROPE_DIM // 2
    inv = (np.float32(ROPE_THETA) ** (-np.arange(half, dtype=np.float32) / np.float32(half))).astype(np.float32)
    pos = np.concatenate([np.arange(N_PROMPT), np.full((N_SAMPLE,), PAST_LEN)]).astype(np.float32)
    ang = (pos[:, None] * inv[None, :]).astype(np.float32)
    c, s = np.cos(ang.astype(np.float64)), np.sin(ang.astype(np.float64))
    return jnp.asarray(np.concatenate([c, c, -s, s], axis=1).astype(np.float32))


def _rot_half_cols(w):
    half = QK_ROPE_DIM // 2
    return jnp.concatenate([w[..., half:], w[..., :half]], axis=-1)


def kernel(x_prompt, x_sample, state_conv, cache_kv_latent, page_table, g_mix, g_ffn, g_final, w_pw1, b_pw1, w_dw,
           b_dw, g_conv_ln, b_conv_ln, w_pw2, b_pw2, g_kv_in, w_dkv, g_ckv, w_uk, w_uv, w_dq, g_q_a, w_uq, w_o,
           w_router_group, b_router_group, w_router_expert, b_router_expert, w_exp_gate, w_exp_up, w_exp_down):
    vec = lambda a: a.reshape(1, -1)
    x = jnp.concatenate([x_prompt[0], x_sample[:, 0]], axis=0)
    tab = _rope_table()

    w1 = w_pw1[0].astype(BF16)
    w2 = w_pw2[0].astype(BF16)
    w_kv_ext = jnp.concatenate([w_dkv, _rot_half_cols(w_dkv[:, KV_LORA_RANK:])], axis=1).astype(BF16)
    wk = w_uk.transpose(1, 0, 2).reshape(KV_LORA_RANK, N_HEADS * QK_NOPE_DIM).astype(BF16)
    wv = w_uv.transpose(1, 0, 2).reshape(KV_LORA_RANK, N_HEADS * V_HEAD_DIM).astype(BF16)
    w_uk_t = w_uk.transpose(0, 2, 1).astype(BF16)
    w_uv_b = w_uv.astype(BF16)
    wdq = w_dq[0].astype(BF16)
    wuq = w_uq[0]
    wuq_ext = jnp.concatenate([wuq, _rot_half_cols(wuq[..., QK_NOPE_DIM:])], axis=-1)
    wuq_ext = wuq_ext.reshape(Q_LORA_RANK, N_HEADS * HEAD_PAD).astype(BF16)
    wo = w_o[0].astype(BF16)
    wr_t = [jnp.concatenate([w_router_group[l], w_router_expert[l]], axis=1).T for l in range(2)]
    br = [jnp.concatenate([b_router_group[l], b_router_expert[l]]).reshape(-1, 1) for l in range(2)]

    glu = _pw1_glu(x, vec(g_mix[0]), w1, vec(b_pw1[0]))
    conv_args = (w_dw[0], vec(b_dw[0]), vec(g_conv_ln[0]), vec(b_conv_ln[0]), w2, vec(b_pw2[0]))
    h1 = _conv_prompt(glu, x, *conv_args)
    h1, state_new_t = _conv_sample(h1, state_conv[0].transpose(1, 0, 2), glu, x, *conv_args)

    y0 = _moe(0, h1, vec(g_ffn[0]), wr_t[0], br[0], w_exp_gate, w_exp_up, w_exp_down)
    h2, kv_rows, k_heads, v_heads = _post_moe0(h1, y0, vec(g_kv_in), w_kv_ext, vec(g_ckv), tab, wk, wv)

    q = _q_proj(h2, vec(g_mix[1]), wdq, vec(g_q_a[0]), wuq_ext, tab)
    o = _attn_prompt(q, k_heads, v_heads)
    q_lat = _q_lat(q, w_uk_t).transpose(1, 0, 2)
    lat = _attn_sample(page_table, q_lat, kv_rows[N_PROMPT:].reshape(N_SAMPLE, 1, LATENT_DIM),
                       cache_kv_latent.transpose(0, 2, 1))
    o = _uv_sample(o, lat.transpose(1, 0, 2), w_uv_b)
    h3 = _out_proj(o, h2, wo)

    y1 = _moe(1, h3, vec(g_ffn[1]), wr_t[1], br[1], w_exp_gate, w_exp_up, w_exp_down)
    out = _final(h3, y1, vec(g_final))

    conv_state_prompt = glu[N_PROMPT - (CONV_WIDTH - 1):N_PROMPT][None, None]
    conv_state_sample = state_new_t.transpose(1, 0, 2)[None]
    return (out[:N_PROMPT][None], out[N_PROMPT:][:, None], conv_state_prompt, conv_state_sample,
            kv_rows[:N_PROMPT][None], kv_rows[N_PROMPT:][:, None])
```

```python
import functools

import jax
import jax.numpy as jnp
import numpy as np
from jax import lax
from jax.experimental import pallas as pl
from jax.experimental.pallas import tpu as pltpu

F32 = jnp.float32
BF16 = jnp.bfloat16

D_MODEL = 2048
N_PROMPT = 8192
N_SAMPLE = 128
N_TOK = N_PROMPT + N_SAMPLE
PAST_LEN = 16384
PAGE_SIZE = 128
N_PAGES = PAST_LEN // PAGE_SIZE
CONV_WIDTH = 31
N_HEADS = 16
QK_NOPE_DIM = 128
QK_ROPE_DIM = 64
V_HEAD_DIM = 128
Q_LORA_RANK = 512
KV_LORA_RANK = 512
LATENT_DIM = KV_LORA_RANK + QK_ROPE_DIM
ROPE_THETA = 10000.0
SOFTMAX_SCALE = (QK_NOPE_DIM + QK_ROPE_DIM) ** -0.5
N_GROUPS = 8
EXPERTS_PER_GROUP = 8
N_EXPERTS = N_GROUPS * EXPERTS_PER_GROUP
EXPERT_FF = 512
NORM_EPS = 1e-6
LN_EPS = 1e-5

LANES = 128
SUBLANES = 8
HEAD_PAD = 2 * LANES
KV_EXT = KV_LORA_RANK + LANES

ROW_TILE = 128
MAX_TILES = 2 * N_TOK // ROW_TILE + N_EXPERTS
N_ITEMS = 2 * N_TOK
TOK_BITS = 14
TOK_MASK = (1 << TOK_BITS) - 1
LOG2E = 1.4426950408889634

VMEM_LIMIT = 56 * 1024 * 1024


def _cparams(sem):
    return pltpu.CompilerParams(dimension_semantics=sem, vmem_limit_bytes=VMEM_LIMIT)


def _rms(x, g):
    return x * lax.rsqrt(jnp.mean(x * x, axis=-1, keepdims=True) + NORM_EPS) * g


def _dot(a, b):
    return jnp.dot(a, b, preferred_element_type=F32)


def _dot_nt(a, b):
    return lax.dot_general(a, b, (((1,), (1,)), ((), ())), preferred_element_type=F32)


PW1_TM = 832
PW1_TN = 512


def _pw1_kernel(x_ref, g_ref, wa_ref, wb_ref, ba_ref, bb_ref, o_ref, xn_ref):
    @pl.when(pl.program_id(1) == 0)
    def _():
        xn_ref[...] = _rms(x_ref[...], g_ref[...]).astype(BF16)

    xn = xn_ref[...]
    a = _dot(xn, wa_ref[...]) + ba_ref[...]
    b = _dot(xn, wb_ref[...]) + bb_ref[...]
    o_ref[...] = a * jax.nn.sigmoid(b)


def _pw1_glu(x, g, w, b):
    nj = D_MODEL // PW1_TN
    return pl.pallas_call(
        _pw1_kernel,
        grid=(N_TOK // PW1_TM, nj),
        in_specs=[
            pl.BlockSpec((PW1_TM, D_MODEL), lambda i, j: (i, 0)),
            pl.BlockSpec((1, D_MODEL), lambda i, j: (0, 0)),
            pl.BlockSpec((D_MODEL, PW1_TN), lambda i, j: (0, j)),
            pl.BlockSpec((D_MODEL, PW1_TN), lambda i, j: (0, j + nj)),
            pl.BlockSpec((1, PW1_TN), lambda i, j: (0, j)),
            pl.BlockSpec((1, PW1_TN), lambda i, j: (0, j + nj)),
        ],
        out_specs=pl.BlockSpec((PW1_TM, PW1_TN), lambda i, j: (i, j)),
        out_shape=jax.ShapeDtypeStruct((N_TOK, D_MODEL), F32),
        scratch_shapes=[pltpu.VMEM((PW1_TM, D_MODEL), BF16)],
        compiler_params=_cparams(("parallel", "arbitrary")),
        name="pw1_glu",
    )(x, g, w, w, b, b)


CONV_TM = 256
HALO = 32
CONV_CW = 256
CONV_RB = 64


def _ln_swish_pw2(dw, x, gln, bln, w2, b2):
    mu = jnp.mean(dw, axis=-1, keepdims=True)
    c = dw - mu
    var = jnp.mean(c * c, axis=-1, keepdims=True)
    y = c * lax.rsqrt(var + LN_EPS) * gln + bln
    s = y * jax.nn.sigmoid(y)
    return x + _dot(s.astype(BF16), w2) + b2


def _conv_prompt_kernel(glu_ref, halo_ref, x_ref, wdw_ref, bdw_ref, gln_ref, bln_ref, w2_ref, b2_ref,
                        o_ref, buf_ref, sh_ref, dw_ref):
    first = pl.program_id(0) == 0
    rows = CONV_TM + HALO
    buf_ref[0:HALO, :] = jnp.where(first, 0.0, halo_ref[...])
    buf_ref[HALO:rows, :] = glu_ref[...]
    buf_ref[rows:, :] = jnp.zeros((SUBLANES, D_MODEL), F32)
    off = HALO - (CONV_WIDTH - 1)

    def chunk(c, carry):
        cols = pl.ds(pl.multiple_of(c * CONV_CW, CONV_CW), CONV_CW)
        for j in range(1, SUBLANES):
            sh_ref[j] = buf_ref[j:j + rows, cols]
        for r0 in range(0, CONV_TM, CONV_RB):
            acc = jnp.broadcast_to(bdw_ref[:, cols], (CONV_RB, CONV_CW))
            for k in range(CONV_WIDTH):
                j, base = (off + k) % SUBLANES, r0 + (off + k) // SUBLANES * SUBLANES
                win = sh_ref[j, base:base + CONV_RB, :] if j else buf_ref[base:base + CONV_RB, cols]
                acc = acc + wdw_ref[k:k + 1, cols] * win
            dw_ref[r0:r0 + CONV_RB, cols] = acc
        return carry

    lax.fori_loop(0, D_MODEL // CONV_CW, chunk, 0)
    o_ref[...] = _ln_swish_pw2(dw_ref[...], x_ref[...], gln_ref[...], bln_ref[...], w2_ref[...], b2_ref[...])


def _conv_prompt(glu, x, wdw, bdw, gln, bln, w2, b2):
    vec = pl.BlockSpec((1, D_MODEL), lambda i: (0, 0))
    return pl.pallas_call(
        _conv_prompt_kernel,
        grid=(N_PROMPT // CONV_TM,),
        in_specs=[
            pl.BlockSpec((CONV_TM, D_MODEL), lambda i: (i, 0)),
            pl.BlockSpec((HALO, D_MODEL), lambda i: (jnp.maximum(i * (CONV_TM // HALO) - 1, 0), 0)),
            pl.BlockSpec((CONV_TM, D_MODEL), lambda i: (i, 0)),
            pl.BlockSpec((CONV_WIDTH, D_MODEL), lambda i: (0, 0)),
            vec, vec, vec,
            pl.BlockSpec((D_MODEL, D_MODEL), lambda i: (0, 0)),
            vec,
        ],
        out_specs=pl.BlockSpec((CONV_TM, D_MODEL), lambda i: (i, 0)),
        out_shape=jax.ShapeDtypeStruct((N_TOK, D_MODEL), F32),
        scratch_shapes=[pltpu.VMEM((CONV_TM + HALO + SUBLANES, D_MODEL), F32),
                        pltpu.VMEM((SUBLANES, CONV_TM + HALO, CONV_CW), F32),
                        pltpu.VMEM((CONV_TM, D_MODEL), F32)],
        compiler_params=_cparams(("arbitrary",)),
        name="conv_prompt",
    )(glu, glu, x, wdw, bdw, gln, bln, w2, b2)


CONV_SCW = 256


def _conv_sample_kernel(h_any, st_ref, glu_ref, x_ref, wdw_ref, bdw_ref, gln_ref, bln_ref, w2_ref, b2_ref,
                        o_ref, cs_ref, dw_ref):
    del h_any
    j = pl.program_id(0)
    hist = CONV_WIDTH - 1
    glu = glu_ref[...]
    acc = bdw_ref[...] + wdw_ref[hist:CONV_WIDTH, :] * glu
    for k in range(hist):
        acc = acc + wdw_ref[k:k + 1, :] * st_ref[k]
    dw_ref[:, pl.ds(pl.multiple_of(j * CONV_SCW, CONV_SCW), CONV_SCW)] = acc
    cs_ref[0:hist - 1] = st_ref[1:hist]
    cs_ref[hist - 1] = glu

    @pl.when(j == D_MODEL // CONV_SCW - 1)
    def _():
        o_ref[...] = _ln_swish_pw2(dw_ref[...], x_ref[...], gln_ref[...], bln_ref[...], w2_ref[...], b2_ref[...])


def _conv_sample(h_buf, state_t, glu, x, wdw, bdw, gln, bln, w2, b2):
    hist = CONV_WIDTH - 1
    vec = pl.BlockSpec((1, D_MODEL), lambda j: (0, 0))
    cvec = pl.BlockSpec((1, CONV_SCW), lambda j: (0, j))
    rows = pl.BlockSpec((N_SAMPLE, D_MODEL), lambda j: (N_PROMPT // N_SAMPLE, 0))
    st_spec = pl.BlockSpec((hist, N_SAMPLE, CONV_SCW), lambda j: (0, 0, j))
    return pl.pallas_call(
        _conv_sample_kernel,
        grid=(D_MODEL // CONV_SCW,),
        in_specs=[
            pl.BlockSpec(memory_space=pl.ANY),
            st_spec,
            pl.BlockSpec((N_SAMPLE, CONV_SCW), lambda j: (N_PROMPT // N_SAMPLE, j)),
            rows,
            pl.BlockSpec((CONV_WIDTH, CONV_SCW), lambda j: (0, j)),
            cvec, vec, vec,
            pl.BlockSpec((D_MODEL, D_MODEL), lambda j: (0, 0)),
            vec,
        ],
        out_specs=[rows, st_spec],
        out_shape=[jax.ShapeDtypeStruct((N_TOK, D_MODEL), F32),
                   jax.ShapeDtypeStruct((hist, N_SAMPLE, D_MODEL), F32)],
        scratch_shapes=[pltpu.VMEM((N_SAMPLE, D_MODEL), F32)],
        input_output_aliases={0: 0},
        compiler_params=_cparams(("arbitrary",)),
        name="conv_sample",
    )(h_buf, state_t, glu, x, wdw, bdw, gln, bln, w2, b2)


ROUTER_TM = 640


def _router_kernel(h_ref, g_ref, wr_ref, br_ref, xn_ref, ids_ref, tw_ref):
    xn = _rms(h_ref[...], g_ref[...])
    xn_ref[...] = xn
    lg = lax.dot_general(wr_ref[...], xn, (((1,), (1,)), ((), ())), precision=lax.Precision.HIGHEST,
                         preferred_element_type=F32) + br_ref[...]
    glog = lg[0:N_GROUPS]
    iota = lax.broadcasted_iota(jnp.int32, (N_GROUPS, ROUTER_TM), 0)
    gmax = jnp.max(glog, axis=0, keepdims=True)
    gi = jnp.min(jnp.where(glog == gmax, iota, N_GROUPS), axis=0, keepdims=True)
    gp = 1.0 / jnp.sum(jnp.exp(glog - gmax), axis=0, keepdims=True)
    eg = lg[N_GROUPS:N_GROUPS + EXPERTS_PER_GROUP]
    for g in range(1, N_GROUPS):
        lo = N_GROUPS + g * EXPERTS_PER_GROUP
        eg = jnp.where(gi == g, lg[lo:lo + EXPERTS_PER_GROUP], eg)
    v1 = jnp.max(eg, axis=0, keepdims=True)
    i1 = jnp.min(jnp.where(eg == v1, iota, EXPERTS_PER_GROUP), axis=0, keepdims=True)
    eg2 = jnp.where(iota == i1, -jnp.inf, eg)
    v2 = jnp.max(eg2, axis=0, keepdims=True)
    i2 = jnp.min(jnp.where(eg2 == v2, iota, EXPERTS_PER_GROUP), axis=0, keepdims=True)
    e2 = jnp.exp(v2 - v1)
    inv = gp / (1.0 + e2)
    ids_ref[...] = jnp.concatenate([gi * EXPERTS_PER_GROUP + i1, gi * EXPERTS_PER_GROUP + i2], axis=0)
    tw_ref[...] = jnp.concatenate([inv, inv * e2], axis=0)


def _router(h, g, wr_t, br):
    n_r = N_GROUPS + N_EXPERTS
    return pl.pallas_call(
        _router_kernel,
        grid=(N_TOK // ROUTER_TM,),
        in_specs=[
            pl.BlockSpec((ROUTER_TM, D_MODEL), lambda i: (i, 0)),
            pl.BlockSpec((1, D_MODEL), lambda i: (0, 0)),
            pl.BlockSpec((n_r, D_MODEL), lambda i: (0, 0)),
            pl.BlockSpec((n_r, 1), lambda i: (0, 0)),
        ],
        out_specs=[
            pl.BlockSpec((ROUTER_TM, D_MODEL), lambda i: (i, 0)),
            pl.BlockSpec((2, ROUTER_TM), lambda i: (0, i)),
            pl.BlockSpec((2, ROUTER_TM), lambda i: (0, i)),
        ],
        out_shape=[
            jax.ShapeDtypeStruct((N_TOK, D_MODEL), F32),
            jax.ShapeDtypeStruct((2, N_TOK), jnp.int32),
            jax.ShapeDtypeStruct((2, N_TOK), F32),
        ],
        compiler_params=_cparams(("parallel",)),
        name="router",
    )(h, g, wr_t, br)


def _dispatch_plan(ids, tw):
    flat_e = ids.reshape(-1)
    tw_bits = lax.bitcast_convert_type(tw.reshape(-1), jnp.int32)
    _, sorted_item, sorted_tw = lax.sort((flat_e, jnp.arange(N_ITEMS, dtype=jnp.int32), tw_bits), num_keys=1)
    eids = jnp.arange(N_EXPERTS, dtype=jnp.int32)
    counts = jnp.sum(flat_e[None, :] == eids[:, None], axis=1, dtype=jnp.int32)
    used = counts > 0
    ntile = (counts + ROW_TILE - 1) // ROW_TILE
    tile_end = jnp.cumsum(ntile)
    n_active = tile_end[-1]
    first_row = jnp.cumsum(counts) - counts
    parity = (jnp.cumsum(used.astype(jnp.int32)) - 1) % 2
    later_used = (eids[None, :] > eids[:, None]) & used[None, :]
    nxt = jnp.min(jnp.where(later_used, eids[None, :], N_EXPERTS), axis=1)
    nxt = jnp.where(nxt == N_EXPERTS, -1, nxt)

    tile_ids = jnp.arange(MAX_TILES + 1, dtype=jnp.int32)
    active = tile_ids < n_active
    tile_e = jnp.minimum(jnp.sum(tile_ids[:, None] >= tile_end[None, :], axis=1, dtype=jnp.int32), N_EXPERTS - 1)
    onehot = tile_e[:, None] == eids[None, :]
    pick = lambda v: jnp.sum(jnp.where(onehot, v[None, :], 0), axis=1, dtype=jnp.int32)
    within = tile_ids - pick(tile_end - ntile)
    cnt = jnp.where(active, jnp.clip(pick(counts) - within * ROW_TILE, 0, ROW_TILE), 0)
    lane = jnp.arange(ROW_TILE, dtype=jnp.int32)[None, :]
    valid = lane < cnt[:, None]
    pos = jnp.clip((pick(first_row) + within * ROW_TILE)[:, None] + lane, 0, N_ITEMS - 1)
    picked = jnp.stack([sorted_item, sorted_tw], axis=1)[pos]
    dump = N_ITEMS + (tile_ids % 2)[:, None] * ROW_TILE + lane
    item = jnp.where(valid, picked[..., 0], dump)
    tok = item - jnp.where(item >= N_TOK, N_TOK, 0) - jnp.where(item >= N_ITEMS, N_TOK, 0)
    roww = jnp.where(valid, lax.bitcast_convert_type(picked[..., 1], F32), 0.0)
    packed = (item << TOK_BITS) | tok
    first = (active & (within == 0)).astype(jnp.int32)
    return (tile_e, first, pick(parity), pick(nxt), cnt, packed.reshape(-1),
            roww[:MAX_TILES].reshape(-1, 1), n_active.reshape(1))


ROW_GROUP = 16
N_ROW_GROUPS = ROW_TILE // ROW_GROUP


def _experts_kernel(layer, te_ref, first_ref, par_ref, next_ref, cnt_ref, item_ref, nact_ref,
                    xn_hbm, roww_ref, wg_hbm, wu_hbm, wd_hbm, y_hbm,
                    xbuf, ybuf, wg_buf, wu_buf, wd_buf, gsem, ssem, wsem):
    t = pl.program_id(0)
    nact = nact_ref[0]
    slot = t % 2

    def for_groups(tile, fn):
        n = cnt_ref[tile]
        for g in range(N_ROW_GROUPS):
            pl.when(n > g * ROW_GROUP)(functools.partial(fn, g))

    def start_gather(tile, sl):
        def group(g):
            for r in range(g * ROW_GROUP, (g + 1) * ROW_GROUP):
                tok = item_ref[tile * ROW_TILE + r] & TOK_MASK
                pltpu.make_async_copy(xn_hbm.at[pl.ds(tok, 1), :], xbuf.at[sl, pl.ds(r, 1), :], gsem.at[sl]).start()
        for_groups(tile, group)

    def start_scatter(tile, sl):
        def group(g):
            for r in range(g * ROW_GROUP, (g + 1) * ROW_GROUP):
                it = item_ref[tile * ROW_TILE + r] >> TOK_BITS
                pltpu.make_async_copy(ybuf.at[sl, pl.ds(r, 1), :], y_hbm.at[pl.ds(it, 1), :], ssem.at[sl]).start()
        for_groups(tile, group)

    def wait_gather(tile, sl):
        def group(g):
            rows = pl.ds(g * ROW_GROUP, ROW_GROUP)
            pltpu.make_async_copy(xn_hbm.at[rows, :], xbuf.at[sl, rows, :], gsem.at[sl]).wait()
        for_groups(tile, group)

    def wait_scatter(tile, sl):
        def group(g):
            rows = pl.ds(g * ROW_GROUP, ROW_GROUP)
            pltpu.make_async_copy(ybuf.at[sl, rows, :], y_hbm.at[rows, :], ssem.at[sl]).wait()
        for_groups(tile, group)

    def weight_copies(e, par):
        idx = (layer, lax.div(e, EXPERTS_PER_GROUP), lax.rem(e, EXPERTS_PER_GROUP))
        return [pltpu.make_async_copy(hbm.at[idx], buf.at[par], wsem.at[par])
                for hbm, buf in ((wg_hbm, wg_buf), (wu_hbm, wu_buf), (wd_hbm, wd_buf))]

    @pl.when((t == 0) & (nact > 0))
    def _():
        xbuf[...] = jnp.zeros(xbuf.shape, F32)
        start_gather(0, 0)
        for cp in weight_copies(te_ref[0], par_ref[0]):
            cp.start()

    @pl.when(t < nact)
    def _():
        par = par_ref[t]

        @pl.when(first_ref[t] == 1)
        def _():
            for cp in weight_copies(te_ref[t], par):
                cp.wait()

            @pl.when(next_ref[t] >= 0)
            def _():
                for cp in weight_copies(next_ref[t], 1 - par):
                    cp.start()

        def tile_body(sl):
            wait_gather(t, sl)

            @pl.when(t >= 2)
            def _():
                wait_scatter(t - 2, sl)

            start_gather(t + 1, 1 - sl)
            x = xbuf[sl].astype(BF16)
            gate = _dot(x, wg_buf[par].astype(BF16))
            up = _dot(x, wu_buf[par].astype(BF16))
            hid = gate * jax.nn.sigmoid(gate) * up * roww_ref[...]
            ybuf[sl] = _dot(hid.astype(BF16), wd_buf[par].astype(BF16))
            start_scatter(t, sl)

            @pl.when(t == nact - 1)
            def _():
                wait_scatter(t, sl)

                @pl.when(t >= 1)
                def _():
                    wait_scatter(t - 1, 1 - sl)

        for sl in range(2):
            pl.when(slot == sl)(functools.partial(tile_body, sl))


def _experts(layer, plan, xn, w_gate, w_up, w_down):
    tile_e, first, par, nxt, cnt, packed, roww, n_active = plan
    any_spec = pl.BlockSpec(memory_space=pl.ANY)
    grid_spec = pltpu.PrefetchScalarGridSpec(
        num_scalar_prefetch=7,
        grid=(MAX_TILES,),
        in_specs=[any_spec, pl.BlockSpec((ROW_TILE, 1), lambda t, *_: (t, 0)), any_spec, any_spec, any_spec],
        out_specs=any_spec,
        scratch_shapes=[
            pltpu.VMEM((2, ROW_TILE, D_MODEL), F32),
            pltpu.VMEM((2, ROW_TILE, D_MODEL), F32),
            pltpu.VMEM((2, D_MODEL, EXPERT_FF), F32),
            pltpu.VMEM((2, D_MODEL, EXPERT_FF), F32),
            pltpu.VMEM((2, EXPERT_FF, D_MODEL), F32),
            pltpu.SemaphoreType.DMA((2,)),
            pltpu.SemaphoreType.DMA((2,)),
            pltpu.SemaphoreType.DMA((2,)),
        ],
    )
    return pl.pallas_call(
        functools.partial(_experts_kernel, layer),
        grid_spec=grid_spec,
        out_shape=jax.ShapeDtypeStruct((N_ITEMS + 2 * ROW_TILE, D_MODEL), F32),
        compiler_params=_cparams(("arbitrary",)),
        name=f"experts{layer}",
    )(tile_e, first, par, nxt, cnt, packed, n_active, xn, roww, w_gate, w_up, w_down)


def _moe(layer, h, g_ffn, wr_t, br, w_gate, w_up, w_down):
    xn, ids, tw = _router(h, g_ffn, wr_t, br)
    return _experts(layer, _dispatch_plan(ids, tw), xn, w_gate, w_up, w_down)


POST_TM = 320


def _rope_slab(slab, tab):
    z = slab * tab
    return z + pltpu.roll(z, LANES // 2, 1)


def _post_moe0_kernel(h_ref, ya_ref, yb_ref, gkv_ref, wkv_ref, gc_ref, tab_ref, wk_ref, wv_ref,
                      h2_ref, kv_ref, k_ref, v_ref):
    h2 = h_ref[...] + ya_ref[...] + yb_ref[...]
    h2_ref[...] = h2
    a = _dot(_rms(h2, gkv_ref[...]).astype(BF16), wkv_ref[...])
    ckv = _rms(a[:, :KV_LORA_RANK], gc_ref[...])
    pe = _rope_slab(a[:, KV_LORA_RANK:], tab_ref[...])
    kv_ref[:, :KV_LORA_RANK] = ckv
    kv_ref[:, KV_LORA_RANK:] = pe[:, :QK_ROPE_DIM]
    lane = lax.broadcasted_iota(jnp.int32, pe.shape, 1)
    pe_pad = jnp.where(lane < QK_ROPE_DIM, pe, 0.0).astype(BF16)
    cb = ckv.astype(BF16)
    kn = _dot(cb, wk_ref[...]).astype(BF16)
    v_ref[...] = _dot(cb, wv_ref[...]).astype(BF16)
    for hd in range(N_HEADS):
        k_ref[:, hd * HEAD_PAD:hd * HEAD_PAD + LANES] = kn[:, hd * LANES:(hd + 1) * LANES]
        k_ref[:, hd * HEAD_PAD + LANES:(hd + 1) * HEAD_PAD] = pe_pad


def _post_moe0(h, y, g_kv, w_kv_ext, g_ckv, tab, wk, wv):
    nb = N_TOK // POST_TM
    row = lambda w: pl.BlockSpec((POST_TM, w), lambda i: (i, 0))
    full = lambda a: pl.BlockSpec(a.shape, lambda i: (0, 0))
    return pl.pallas_call(
        _post_moe0_kernel,
        grid=(nb,),
        in_specs=[row(D_MODEL), row(D_MODEL), pl.BlockSpec((POST_TM, D_MODEL), lambda i: (i + nb, 0)),
                  full(g_kv), full(w_kv_ext), full(g_ckv), row(LANES), full(wk), full(wv)],
        out_specs=[row(D_MODEL), row(LATENT_DIM), row(N_HEADS * HEAD_PAD), row(N_HEADS * V_HEAD_DIM)],
        out_shape=[
            jax.ShapeDtypeStruct((N_TOK, D_MODEL), F32),
            jax.ShapeDtypeStruct((N_TOK, LATENT_DIM), F32),
            jax.ShapeDtypeStruct((N_TOK, N_HEADS * HEAD_PAD), BF16),
            jax.ShapeDtypeStruct((N_TOK, N_HEADS * V_HEAD_DIM), BF16),
        ],
        compiler_params=_cparams(("parallel",)),
        name="post_moe0",
    )(h, y, y, g_kv, w_kv_ext, g_ckv, tab, wk, wv)


def _q_proj_kernel(h_ref, g_ref, wdq_ref, gq_ref, wuq_ref, tab_ref, q_ref):
    xn = _rms(h_ref[...], g_ref[...]).astype(BF16)
    cq = _rms(_dot(xn, wdq_ref[...]), gq_ref[...]).astype(BF16)
    q = _dot(cq, wuq_ref[...]) * (SOFTMAX_SCALE * LOG2E)
    tab = tab_ref[...]
    for hd in range(N_HEADS):
        lo = hd * HEAD_PAD
        q_ref[:, lo:lo + LANES] = q[:, lo:lo + LANES].astype(BF16)
        q_ref[:, lo + LANES:lo + HEAD_PAD] = _rope_slab(q[:, lo + LANES:lo + HEAD_PAD], tab).astype(BF16)


def _q_proj(h, g, wdq, gq, wuq_ext, tab):
    row = lambda w: pl.BlockSpec((POST_TM, w), lambda i: (i, 0))
    full = lambda a: pl.BlockSpec(a.shape, lambda i: (0, 0))
    return pl.pallas_call(
        _q_proj_kernel,
        grid=(N_TOK // POST_TM,),
        in_specs=[row(D_MODEL), full(g), full(wdq), full(gq), full(wuq_ext), row(LANES)],
        out_specs=row(N_HEADS * HEAD_PAD),
        out_shape=jax.ShapeDtypeStruct((N_TOK, N_HEADS * HEAD_PAD), BF16),
        compiler_params=_cparams(("parallel",)),
        name="q_proj",
    )(h, g, wdq, gq, wuq_ext, tab)


ATT_T = 1024
ATT_NB = N_PROMPT // ATT_T
ATT_RC = 256
_PAIRS = [(qi, kj) for qi in range(ATT_NB) for kj in range(qi + 1)]
_PAIR_Q = np.array([p[0] for p in _PAIRS], np.int32)
_PAIR_K = np.array([p[1] for p in _PAIRS], np.int32)


def _attn_tile(q_ref, k_ref, v_ref, m_ref, acc_ref, diagonal):
    k = k_ref[...]
    v_ext = jnp.concatenate([v_ref[...], jnp.ones((ATT_T, LANES), BF16)], axis=1)
    for r0 in range(0, ATT_T, ATT_RC):
        rs = slice(r0, r0 + ATT_RC)
        s = _dot_nt(q_ref[rs, :], k)
        if diagonal:
            rows = r0 + lax.broadcasted_iota(jnp.int32, s.shape, 0)
            cols = lax.broadcasted_iota(jnp.int32, s.shape, 1)
            s = jnp.where(cols <= rows, s, -jnp.inf)
        m_old = m_ref[rs, :]
        m_new = jnp.maximum(m_old, jnp.max(s, axis=-1, keepdims=True))
        p = jnp.exp2(s - m_new).astype(BF16)
        acc_ref[rs, :] = acc_ref[rs, :] * jnp.exp2(m_old - m_new) + _dot(p, v_ext)
        m_ref[rs, :] = m_new


def _attn_prompt_kernel(pq_ref, pk_ref, q_ref, k_ref, v_ref, o_ref, m_ref, acc_ref):
    p = pl.program_id(1)
    qi = pq_ref[p]
    kj = pk_ref[p]

    @pl.when(kj == 0)
    def _():
        m_ref[...] = jnp.full(m_ref.shape, -jnp.inf, F32)
        acc_ref[...] = jnp.zeros(acc_ref.shape, F32)

    @pl.when(kj < qi)
    def _():
        _attn_tile(q_ref, k_ref, v_ref, m_ref, acc_ref, diagonal=False)

    @pl.when(kj == qi)
    def _():
        _attn_tile(q_ref, k_ref, v_ref, m_ref, acc_ref, diagonal=True)
        acc = acc_ref[...]
        o_ref[...] = (acc[:, :V_HEAD_DIM] / acc[:, V_HEAD_DIM:]).astype(BF16)


def _attn_prompt(q, k, v):
    grid_spec = pltpu.PrefetchScalarGridSpec(
        num_scalar_prefetch=2,
        grid=(N_HEADS, len(_PAIRS)),
        in_specs=[
            pl.BlockSpec((ATT_T, HEAD_PAD), lambda h, p, pq, pk: (pq[p], h)),
            pl.BlockSpec((ATT_T, HEAD_PAD), lambda h, p, pq, pk: (pk[p], h)),
            pl.BlockSpec((ATT_T, V_HEAD_DIM), lambda h, p, pq, pk: (pk[p], h)),
        ],
        out_specs=pl.BlockSpec((ATT_T, V_HEAD_DIM), lambda h, p, pq, pk: (pq[p], h)),
        scratch_shapes=[
            pltpu.VMEM((ATT_T, 1), F32),
            pltpu.VMEM((ATT_T, V_HEAD_DIM + LANES), F32),
        ],
    )
    return pl.pallas_call(
        _attn_prompt_kernel,
        grid_spec=grid_spec,
        out_shape=jax.ShapeDtypeStruct((N_TOK, N_HEADS * V_HEAD_DIM), BF16),
        compiler_params=_cparams(("parallel", "arbitrary")),
        name="attn_prompt",
    )(jnp.asarray(_PAIR_Q), jnp.asarray(_PAIR_K), q, k, v)


def _q_lat_kernel(q_ref, wuk_ref, o_ref):
    q = q_ref[...]
    o_ref[:, :KV_LORA_RANK] = _dot(q[:, :QK_NOPE_DIM], wuk_ref[...]).astype(BF16)
    o_ref[:, KV_LORA_RANK:] = q[:, QK_NOPE_DIM:]


def _q_lat(q, w_uk_t):
    return pl.pallas_call(
        _q_lat_kernel,
        grid=(N_HEADS,),
        in_specs=[
            pl.BlockSpec((N_SAMPLE, HEAD_PAD), lambda h: (N_PROMPT // N_SAMPLE, h)),
            pl.BlockSpec((None, QK_NOPE_DIM, KV_LORA_RANK), lambda h: (h, 0, 0)),
        ],
        out_specs=pl.BlockSpec((None, N_SAMPLE, KV_EXT), lambda h: (h, 0, 0)),
        out_shape=jax.ShapeDtypeStruct((N_HEADS, N_SAMPLE, KV_EXT), BF16),
        compiler_params=_cparams(("parallel",)),
        name="q_lat",
    )(q, w_uk_t)


DEC_PAGES = 16
DEC_BB = 2
DEC_CHUNKS = N_PAGES // DEC_PAGES
DEC_STEPS = (N_SAMPLE // DEC_BB) * DEC_CHUNKS


def _attn_sample_kernel(pt_ref, q_ref, kvn_ref, cache_hbm, o_ref, kvbuf, sems, m_ref, l_ref, acc_ref):
    c = pl.program_id(1)
    step = pl.program_id(0) * DEC_CHUNKS + c
    slot = step % 2

    def page_copy(st, sl, j, i):
        seq = lax.div(st, DEC_CHUNKS) * DEC_BB + j
        pg = pt_ref[seq * N_PAGES + lax.rem(st, DEC_CHUNKS) * DEC_PAGES + i]
        return pltpu.make_async_copy(cache_hbm.at[pg], kvbuf.at[sl, j, i], sems.at[sl])

    def start_all(st, sl):
        for j in range(DEC_BB):
            for i in range(DEC_PAGES):
                page_copy(st, sl, j, i).start()

    @pl.when(step == 0)
    def _():
        start_all(0, 0)

    @pl.when(step + 1 < DEC_STEPS)
    def _():
        start_all(step + 1, 1 - slot)

    for j in range(DEC_BB):
        for i in range(DEC_PAGES):
            page_copy(step, slot, j, i).wait()

    for j in range(DEC_BB):
        q = q_ref[j][:, :LATENT_DIM]

        @pl.when(c == 0)
        def _():
            kvn = kvn_ref[j].astype(BF16).astype(F32)
            m_ref[j] = jnp.sum(q.astype(F32) * kvn, axis=-1, keepdims=True)
            l_ref[j] = jnp.ones((N_HEADS, 1), F32)
            acc_ref[j] = jnp.broadcast_to(kvn[:, :KV_LORA_RANK], (N_HEADS, KV_LORA_RANK))

    for j in range(DEC_BB):
        q = q_ref[j][:, :LATENT_DIM]
        kv_t = jnp.concatenate([kvbuf[slot, j, i].astype(BF16) for i in range(DEC_PAGES)], axis=1)
        s = _dot(q, kv_t)
        m_old = m_ref[j]
        m_new = jnp.maximum(m_old, jnp.max(s, axis=-1, keepdims=True))
        corr = jnp.exp2(m_old - m_new)
        e = jnp.exp2(s - m_new)
        l_ref[j] = l_ref[j] * corr + jnp.sum(e, axis=-1, keepdims=True)
        acc_ref[j] = acc_ref[j] * corr + _dot_nt(e.astype(BF16), kv_t[:KV_LORA_RANK])
        m_ref[j] = m_new

    @pl.when(c == DEC_CHUNKS - 1)
    def _():
        o_ref[...] = acc_ref[...] / l_ref[...]


def _attn_sample(page_table, q_lat, kv_new, cache_t):
    grid_spec = pltpu.PrefetchScalarGridSpec(
        num_scalar_prefetch=1,
        grid=(N_SAMPLE // DEC_BB, DEC_CHUNKS),
        in_specs=[
            pl.BlockSpec((DEC_BB, N_HEADS, KV_EXT), lambda b, c, pt: (b, 0, 0)),
            pl.BlockSpec((DEC_BB, 1, LATENT_DIM), lambda b, c, pt: (b, 0, 0)),
            pl.BlockSpec(memory_space=pl.ANY),
        ],
        out_specs=pl.BlockSpec((DEC_BB, N_HEADS, KV_LORA_RANK), lambda b, c, pt: (b, 0, 0)),
        scratch_shapes=[
            pltpu.VMEM((2, DEC_BB, DEC_PAGES, LATENT_DIM, PAGE_SIZE), F32),
            pltpu.SemaphoreType.DMA((2,)),
            pltpu.VMEM((DEC_BB, N_HEADS, 1), F32),
            pltpu.VMEM((DEC_BB, N_HEADS, 1), F32),
            pltpu.VMEM((DEC_BB, N_HEADS, KV_LORA_RANK), F32),
        ],
    )
    return pl.pallas_call(
        _attn_sample_kernel,
        grid_spec=grid_spec,
        out_shape=jax.ShapeDtypeStruct((N_SAMPLE, N_HEADS, KV_LORA_RANK), F32),
        compiler_params=_cparams(("arbitrary", "arbitrary")),
        name="attn_sample",
    )(page_table.reshape(-1), q_lat, kv_new, cache_t)


def _uv_sample_kernel(o_any, lat_ref, wuv_ref, o_ref):
    del o_any
    o_ref[...] = _dot(lat_ref[...].astype(BF16), wuv_ref[...]).astype(BF16)


def _uv_sample(o_buf, lat_t, w_uv):
    return pl.pallas_call(
        _uv_sample_kernel,
        grid=(N_HEADS,),
        in_specs=[
            pl.BlockSpec(memory_space=pl.ANY),
            pl.BlockSpec((None, N_SAMPLE, KV_LORA_RANK), lambda h: (h, 0, 0)),
            pl.BlockSpec((None, KV_LORA_RANK, V_HEAD_DIM), lambda h: (h, 0, 0)),
        ],
        out_specs=pl.BlockSpec((N_SAMPLE, V_HEAD_DIM), lambda h: (N_PROMPT // N_SAMPLE, h)),
        out_shape=jax.ShapeDtypeStruct((N_TOK, N_HEADS * V_HEAD_DIM), BF16),
        input_output_aliases={0: 0},
        compiler_params=_cparams(("arbitrary",)),
        name="uv_sample",
    )(o_buf, lat_t, w_uv)


def _out_proj_kernel(o_ref, h_ref, w_ref, h3_ref):
    h3_ref[...] = h_ref[...] + _dot(o_ref[...], w_ref[...])


def _out_proj(o, h, w_o):
    row = pl.BlockSpec((POST_TM, D_MODEL), lambda i: (i, 0))
    return pl.pallas_call(
        _out_proj_kernel,
        grid=(N_TOK // POST_TM,),
        in_specs=[row, row, pl.BlockSpec((D_MODEL, D_MODEL), lambda i: (0, 0))],
        out_specs=row,
        out_shape=jax.ShapeDtypeStruct((N_TOK, D_MODEL), F32),
        compiler_params=_cparams(("parallel",)),
        name="out_proj",
    )(o, h, w_o)


def _final_kernel(h_ref, ya_ref, yb_ref, g_ref, o_ref):
    o_ref[...] = _rms(h_ref[...] + ya_ref[...] + yb_ref[...], g_ref[...])


def _final(h, y, g):
    nb = N_TOK // POST_TM
    row = pl.BlockSpec((POST_TM, D_MODEL), lambda i: (i, 0))
    return pl.pallas_call(
        _final_kernel,
        grid=(nb,),
        in_specs=[row, row, pl.BlockSpec((POST_TM, D_MODEL), lambda i: (i + nb, 0)),
                  pl.BlockSpec((1, D_MODEL), lambda i: (0, 0))],
        out_specs=row,
        out_shape=jax.ShapeDtypeStruct((N_TOK, D_MODEL), F32),
        compiler_params=_cparams(("parallel",)),
        name="final",
    )(h, y, y, g)


def _rope_table():
    half = QK_ROPE_DIM // 2
    inv = (np.float32(ROPE_THETA) ** (-np.arange(half, dtype=np.float32) / np.float32(half))).astype(np.float32)
    pos = np.concatenate([np.arange(N_PROMPT), np.full((N_SAMPLE,), PAST_LEN)]).astype(np.float32)
    ang = (pos[:, None] * inv[None, :]).astype(np.float32)
    c, s = np.cos(ang.astype(np.float64)), np.sin(ang.astype(np.float64))
    return jnp.asarray(np.concatenate([c, c, -s, s], axis=1).astype(np.float32))


def _rot_half_cols(w):
    half = QK_ROPE_DIM // 2
    return jnp.concatenate([w[..., half:], w[..., :half]], axis=-1)


def kernel(x_prompt, x_sample, state_conv, cache_kv_latent, page_table, g_mix, g_ffn, g_final, w_pw1, b_pw1, w_dw,
           b_dw, g_conv_ln, b_conv_ln, w_pw2, b_pw2, g_kv_in, w_dkv, g_ckv, w_uk, w_uv, w_dq, g_q_a, w_uq, w_o,
           w_router_group, b_router_group, w_router_expert, b_router_expert, w_exp_gate, w_exp_up, w_exp_down):
    vec = lambda a: a.reshape(1, -1)
    x = jnp.concatenate([x_prompt[0], x_sample[:, 0]], axis=0)
    tab = _rope_table()

    w1 = w_pw1[0].astype(BF16)
    w2 = w_pw2[0].astype(BF16)
    w_kv_ext = jnp.concatenate([w_dkv, _rot_half_cols(w_dkv[:, KV_LORA_RANK:])], axis=1).astype(BF16)
    wk = w_uk.transpose(1, 0, 2).reshape(KV_LORA_RANK, N_HEADS * QK_NOPE_DIM).astype(BF16)
    wv = w_uv.transpose(1, 0, 2).reshape(KV_LORA_RANK, N_HEADS * V_HEAD_DIM).astype(BF16)
    w_uk_t = w_uk.transpose(0, 2, 1).astype(BF16)
    w_uv_b = w_uv.astype(BF16)
    wdq = w_dq[0].astype(BF16)
    wuq = w_uq[0]
    wuq_ext = jnp.concatenate([wuq, _rot_half_cols(wuq[..., QK_NOPE_DIM:])], axis=-1)
    wuq_ext = wuq_ext.reshape(Q_LORA_RANK, N_HEADS * HEAD_PAD).astype(BF16)
    wo = w_o[0].astype(BF16)
    wr_t = [jnp.concatenate([w_router_group[l], w_router_expert[l]], axis=1).T for l in range(2)]
    br = [jnp.concatenate([b_router_group[l], b_router_expert[l]]).reshape(-1, 1) for l in range(2)]

    glu = _pw1_glu(x, vec(g_mix[0]), w1, vec(b_pw1[0]))
    conv_args = (w_dw[0], vec(b_dw[0]), vec(g_conv_ln[0]), vec(b_conv_ln[0]), w2, vec(b_pw2[0]))
    h1 = _conv_prompt(glu, x, *conv_args)
    h1, state_new_t = _conv_sample(h1, state_conv[0].transpose(1, 0, 2), glu, x, *conv_args)

    y0 = _moe(0, h1, vec(g_ffn[0]), wr_t[0], br[0], w_exp_gate, w_exp_up, w_exp_down)
    h2, kv_rows, k_heads, v_heads = _post_moe0(h1, y0, vec(g_kv_in), w_kv_ext, vec(g_ckv), tab, wk, wv)

    q = _q_proj(h2, vec(g_mix[1]), wdq, vec(g_q_a[0]), wuq_ext, tab)
    o = _attn_prompt(q, k_heads, v_heads)
    q_lat = _q_lat(q, w_uk_t).transpose(1, 0, 2)
    lat = _attn_sample(page_table, q_lat, kv_rows[N_PROMPT:].reshape(N_SAMPLE, 1, LATENT_DIM),
                       cache_kv_latent.transpose(0, 2, 1))
    o = _uv_sample(o, lat.transpose(1, 0, 2), w_uv_b)
    h3 = _out_proj(o, h2, wo)

    y1 = _moe(1, h3, vec(g_ffn[1]), wr_t[1], br[1], w_exp_gate, w_exp_up, w_exp_down)
    out = _final(h3, y1, vec(g_final))

    conv_state_prompt = glu[N_PROMPT - (CONV_WIDTH - 1):N_PROMPT][None, None]
    conv_state_sample = state_new_t.transpose(1, 0, 2)[None]
    return (out[:N_PROMPT][None], out[N_PROMPT:][:, None], conv_state_prompt, conv_state_sample,
            kv_rows[:N_PROMPT][None], kv_rows[N_PROMPT:][:, None])
```

```python
import functools

import jax
import jax.numpy as jnp
import numpy as np
from jax import lax
from jax.experimental import pallas as pl
from jax.experimental.pallas import tpu as pltpu

F32 = jnp.float32
BF16 = jnp.bfloat16

D_MODEL = 2048
N_PROMPT = 8192
N_SAMPLE = 128
N_TOK = N_PROMPT + N_SAMPLE
PAST_LEN = 16384
PAGE_SIZE = 128
N_PAGES = PAST_LEN // PAGE_SIZE
CONV_WIDTH = 31
N_HEADS = 16
QK_NOPE_DIM = 128
QK_ROPE_DIM = 64
V_HEAD_DIM = 128
Q_LORA_RANK = 512
KV_LORA_RANK = 512
LATENT_DIM = KV_LORA_RANK + QK_ROPE_DIM
ROPE_THETA = 10000.0
SOFTMAX_SCALE = (QK_NOPE_DIM + QK_ROPE_DIM) ** -0.5
N_GROUPS = 8
EXPERTS_PER_GROUP = 8
N_EXPERTS = N_GROUPS * EXPERTS_PER_GROUP
EXPERT_FF = 512
NORM_EPS = 1e-6
LN_EPS = 1e-5

LANES = 128
SUBLANES = 8
HEAD_PAD = 2 * LANES
KV_EXT = KV_LORA_RANK + LANES

ROW_TILE = 128
MAX_TILES = 2 * N_TOK // ROW_TILE + N_EXPERTS
N_ITEMS = 2 * N_TOK
TOK_BITS = 14
TOK_MASK = (1 << TOK_BITS) - 1
LOG2E = 1.4426950408889634

VMEM_LIMIT = 56 * 1024 * 1024


def _cparams(sem):
    return pltpu.CompilerParams(dimension_semantics=sem, vmem_limit_bytes=VMEM_LIMIT)


def _rms(x, g):
    return x * lax.rsqrt(jnp.mean(x * x, axis=-1, keepdims=True) + NORM_EPS) * g


def _dot(a, b):
    return jnp.dot(a, b, preferred_element_type=F32)


def _dot_nt(a, b):
    return lax.dot_general(a, b, (((1,), (1,)), ((), ())), preferred_element_type=F32)


PW1_TM = 832
PW1_TN = 512


def _pw1_kernel(x_ref, g_ref, wa_ref, wb_ref, ba_ref, bb_ref, o_ref, xn_ref):
    @pl.when(pl.program_id(1) == 0)
    def _():
        xn_ref[...] = _rms(x_ref[...], g_ref[...]).astype(BF16)

    xn = xn_ref[...]
    a = _dot(xn, wa_ref[...]) + ba_ref[...]
    b = _dot(xn, wb_ref[...]) + bb_ref[...]
    o_ref[...] = a * jax.nn.sigmoid(b)


def _pw1_glu(x, g, w, b):
    nj = D_MODEL // PW1_TN
    return pl.pallas_call(
        _pw1_kernel,
        grid=(N_TOK // PW1_TM, nj),
        in_specs=[
            pl.BlockSpec((PW1_TM, D_MODEL), lambda i, j: (i, 0)),
            pl.BlockSpec((1, D_MODEL), lambda i, j: (0, 0)),
            pl.BlockSpec((D_MODEL, PW1_TN), lambda i, j: (0, j)),
            pl.BlockSpec((D_MODEL, PW1_TN), lambda i, j: (0, j + nj)),
            pl.BlockSpec((1, PW1_TN), lambda i, j: (0, j)),
            pl.BlockSpec((1, PW1_TN), lambda i, j: (0, j + nj)),
        ],
        out_specs=pl.BlockSpec((PW1_TM, PW1_TN), lambda i, j: (i, j)),
        out_shape=jax.ShapeDtypeStruct((N_TOK, D_MODEL), F32),
        scratch_shapes=[pltpu.VMEM((PW1_TM, D_MODEL), BF16)],
        compiler_params=_cparams(("parallel", "arbitrary")),
        name="pw1_glu",
    )(x, g, w, w, b, b)


CONV_TM = 256
HALO = 32
CONV_CW = 256
CONV_RB = 64


def _ln_swish_pw2(dw, x, gln, bln, w2, b2):
    mu = jnp.mean(dw, axis=-1, keepdims=True)
    c = dw - mu
    var = jnp.mean(c * c, axis=-1, keepdims=True)
    y = c * lax.rsqrt(var + LN_EPS) * gln + bln
    s = y * jax.nn.sigmoid(y)
    return x + _dot(s.astype(BF16), w2) + b2


def _conv_prompt_kernel(glu_ref, halo_ref, x_ref, wdw_ref, bdw_ref, gln_ref, bln_ref, w2_ref, b2_ref,
                        o_ref, buf_ref, sh_ref, dw_ref):
    first = pl.program_id(0) == 0
    rows = CONV_TM + HALO
    buf_ref[0:HALO, :] = jnp.where(first, 0.0, halo_ref[...])
    buf_ref[HALO:rows, :] = glu_ref[...]
    buf_ref[rows:, :] = jnp.zeros((SUBLANES, D_MODEL), F32)
    off = HALO - (CONV_WIDTH - 1)

    def chunk(c, carry):
        cols = pl.ds(pl.multiple_of(c * CONV_CW, CONV_CW), CONV_CW)
        for j in range(1, SUBLANES):
            sh_ref[j] = buf_ref[j:j + rows, cols]
        for r0 in range(0, CONV_TM, CONV_RB):
            acc = jnp.broadcast_to(bdw_ref[:, cols], (CONV_RB, CONV_CW))
            for k in range(CONV_WIDTH):
                j, base = (off + k) % SUBLANES, r0 + (off + k) // SUBLANES * SUBLANES
                win = sh_ref[j, base:base + CONV_RB, :] if j else buf_ref[base:base + CONV_RB, cols]
                acc = acc + wdw_ref[k:k + 1, cols] * win
            dw_ref[r0:r0 + CONV_RB, cols] = acc
        return carry

    lax.fori_loop(0, D_MODEL // CONV_CW, chunk, 0)
    o_ref[...] = _ln_swish_pw2(dw_ref[...], x_ref[...], gln_ref[...], bln_ref[...], w2_ref[...], b2_ref[...])


def _conv_prompt(glu, x, wdw, bdw, gln, bln, w2, b2):
    vec = pl.BlockSpec((1, D_MODEL), lambda i: (0, 0))
    return pl.pallas_call(
        _conv_prompt_kernel,
        grid=(N_PROMPT // CONV_TM,),
        in_specs=[
            pl.BlockSpec((CONV_TM, D_MODEL), lambda i: (i, 0)),
            pl.BlockSpec((HALO, D_MODEL), lambda i: (jnp.maximum(i * (CONV_TM // HALO) - 1, 0), 0)),
            pl.BlockSpec((CONV_TM, D_MODEL), lambda i: (i, 0)),
            pl.BlockSpec((CONV_WIDTH, D_MODEL), lambda i: (0, 0)),
            vec, vec, vec,
            pl.BlockSpec((D_MODEL, D_MODEL), lambda i: (0, 0)),
            vec,
        ],
        out_specs=pl.BlockSpec((CONV_TM, D_MODEL), lambda i: (i, 0)),
        out_shape=jax.ShapeDtypeStruct((N_TOK, D_MODEL), F32),
        scratch_shapes=[pltpu.VMEM((CONV_TM + HALO + SUBLANES, D_MODEL), F32),
                        pltpu.VMEM((SUBLANES, CONV_TM + HALO, CONV_CW), F32),
                        pltpu.VMEM((CONV_TM, D_MODEL), F32)],
        compiler_params=_cparams(("arbitrary",)),
        name="conv_prompt",
    )(glu, glu, x, wdw, bdw, gln, bln, w2, b2)


CONV_SCW = 256


def _conv_sample_kernel(h_any, st_ref, glu_ref, x_ref, wdw_ref, bdw_ref, gln_ref, bln_ref, w2_ref, b2_ref,
                        o_ref, cs_ref, dw_ref):
    del h_any
    j = pl.program_id(0)
    hist = CONV_WIDTH - 1
    glu = glu_ref[...]
    acc = bdw_ref[...] + wdw_ref[hist:CONV_WIDTH, :] * glu
    for k in range(hist):
        acc = acc + wdw_ref[k:k + 1, :] * st_ref[k]
    dw_ref[:, pl.ds(pl.multiple_of(j * CONV_SCW, CONV_SCW), CONV_SCW)] = acc
    cs_ref[0:hist - 1] = st_ref[1:hist]
    cs_ref[hist - 1] = glu

    @pl.when(j == D_MODEL // CONV_SCW - 1)
    def _():
        o_ref[...] = _ln_swish_pw2(dw_ref[...], x_ref[...], gln_ref[...], bln_ref[...], w2_ref[...], b2_ref[...])


def _conv_sample(h_buf, state_t, glu, x, wdw, bdw, gln, bln, w2, b2):
    hist = CONV_WIDTH - 1
    vec = pl.BlockSpec((1, D_MODEL), lambda j: (0, 0))
    cvec = pl.BlockSpec((1, CONV_SCW), lambda j: (0, j))
    rows = pl.BlockSpec((N_SAMPLE, D_MODEL), lambda j: (N_PROMPT // N_SAMPLE, 0))
    st_spec = pl.BlockSpec((hist, N_SAMPLE, CONV_SCW), lambda j: (0, 0, j))
    return pl.pallas_call(
        _conv_sample_kernel,
        grid=(D_MODEL // CONV_SCW,),
        in_specs=[
            pl.BlockSpec(memory_space=pl.ANY),
            st_spec,
            pl.BlockSpec((N_SAMPLE, CONV_SCW), lambda j: (N_PROMPT // N_SAMPLE, j)),
            rows,
            pl.BlockSpec((CONV_WIDTH, CONV_SCW), lambda j: (0, j)),
            cvec, vec, vec,
            pl.BlockSpec((D_MODEL, D_MODEL), lambda j: (0, 0)),
            vec,
        ],
        out_specs=[rows, st_spec],
        out_shape=[jax.ShapeDtypeStruct((N_TOK, D_MODEL), F32),
                   jax.ShapeDtypeStruct((hist, N_SAMPLE, D_MODEL), F32)],
        scratch_shapes=[pltpu.VMEM((N_SAMPLE, D_MODEL), F32)],
        input_output_aliases={0: 0},
        compiler_params=_cparams(("arbitrary",)),
        name="conv_sample",
    )(h_buf, state_t, glu, x, wdw, bdw, gln, bln, w2, b2)


ROUTER_TM = 640


def _router_kernel(h_ref, g_ref, wr_ref, br_ref, xn_ref, ids_ref, tw_ref):
    xn = _rms(h_ref[...], g_ref[...])
    xn_ref[...] = xn
    lg = lax.dot_general(wr_ref[...], xn, (((1,), (1,)), ((), ())), precision=lax.Precision.HIGHEST,
                         preferred_element_type=F32) + br_ref[...]
    glog = lg[0:N_GROUPS]
    iota = lax.broadcasted_iota(jnp.int32, (N_GROUPS, ROUTER_TM), 0)
    gmax = jnp.max(glog, axis=0, keepdims=True)
    gi = jnp.min(jnp.where(glog == gmax, iota, N_GROUPS), axis=0, keepdims=True)
    gp = 1.0 / jnp.sum(jnp.exp(glog - gmax), axis=0, keepdims=True)
    eg = lg[N_GROUPS:N_GROUPS + EXPERTS_PER_GROUP]
    for g in range(1, N_GROUPS):
        lo = N_GROUPS + g * EXPERTS_PER_GROUP
        eg = jnp.where(gi == g, lg[lo:lo + EXPERTS_PER_GROUP], eg)
    v1 = jnp.max(eg, axis=0, keepdims=True)
    i1 = jnp.min(jnp.where(eg == v1, iota, EXPERTS_PER_GROUP), axis=0, keepdims=True)
    eg2 = jnp.where(iota == i1, -jnp.inf, eg)
    v2 = jnp.max(eg2, axis=0, keepdims=True)
    i2 = jnp.min(jnp.where(eg2 == v2, iota, EXPERTS_PER_GROUP), axis=0, keepdims=True)
    e2 = jnp.exp(v2 - v1)
    inv = gp / (1.0 + e2)
    ids_ref[...] = jnp.concatenate([gi * EXPERTS_PER_GROUP + i1, gi * EXPERTS_PER_GROUP + i2], axis=0)
    tw_ref[...] = jnp.concatenate([inv, inv * e2], axis=0)


def _router(h, g, wr_t, br):
    n_r = N_GROUPS + N_EXPERTS
    return pl.pallas_call(
        _router_kernel,
        grid=(N_TOK // ROUTER_TM,),
        in_specs=[
            pl.BlockSpec((ROUTER_TM, D_MODEL), lambda i: (i, 0)),
            pl.BlockSpec((1, D_MODEL), lambda i: (0, 0)),
            pl.BlockSpec((n_r, D_MODEL), lambda i: (0, 0)),
            pl.BlockSpec((n_r, 1), lambda i: (0, 0)),
        ],
        out_specs=[
            pl.BlockSpec((ROUTER_TM, D_MODEL), lambda i: (i, 0)),
            pl.BlockSpec((2, ROUTER_TM), lambda i: (0, i)),
            pl.BlockSpec((2, ROUTER_TM), lambda i: (0, i)),
        ],
        out_shape=[
            jax.ShapeDtypeStruct((N_TOK, D_MODEL), F32),
            jax.ShapeDtypeStruct((2, N_TOK), jnp.int32),
            jax.ShapeDtypeStruct((2, N_TOK), F32),
        ],
        compiler_params=_cparams(("parallel",)),
        name="router",
    )(h, g, wr_t, br)


def _dispatch_plan(ids):
    flat_e = ids.reshape(-1)
    _, sorted_item = lax.sort((flat_e, jnp.arange(N_ITEMS, dtype=jnp.int32)), num_keys=1)
    sorted_tok = sorted_item - jnp.where(sorted_item >= N_TOK, N_TOK, 0)
    packed = jnp.pad((sorted_item << TOK_BITS) | sorted_tok, (0, ROW_TILE))
    eids = jnp.arange(N_EXPERTS, dtype=jnp.int32)
    counts = jnp.sum(flat_e[None, :] == eids[:, None], axis=1, dtype=jnp.int32)
    used = counts > 0
    ntile = (counts + ROW_TILE - 1) // ROW_TILE
    tile_end = jnp.cumsum(ntile)
    n_active = tile_end[-1]
    first_row = jnp.cumsum(counts) - counts
    parity = (jnp.cumsum(used.astype(jnp.int32)) - 1) % 2
    later_used = (eids[None, :] > eids[:, None]) & used[None, :]
    nxt = jnp.min(jnp.where(later_used, eids[None, :], N_EXPERTS), axis=1)
    nxt = jnp.where(nxt == N_EXPERTS, -1, nxt)

    tile_ids = jnp.arange(MAX_TILES + 1, dtype=jnp.int32)
    active = tile_ids < n_active
    tile_e = jnp.minimum(jnp.sum(tile_ids[:, None] >= tile_end[None, :], axis=1, dtype=jnp.int32), N_EXPERTS - 1)
    onehot = tile_e[:, None] == eids[None, :]
    pick = lambda v: jnp.sum(jnp.where(onehot, v[None, :], 0), axis=1, dtype=jnp.int32)
    within = tile_ids - pick(tile_end - ntile)
    cnt = jnp.where(active, jnp.clip(pick(counts) - within * ROW_TILE, 0, ROW_TILE), 0)
    row0 = jnp.where(active, pick(first_row) + within * ROW_TILE, 0)
    first = (active & (within == 0)).astype(jnp.int32)
    return tile_e, first, pick(parity), pick(nxt), row0, cnt, packed, n_active.reshape(1)


ROW_GROUP = 16
N_ROW_GROUPS = ROW_TILE // ROW_GROUP


def _experts_kernel(layer, te_ref, first_ref, par_ref, next_ref, row0_ref, cnt_ref, item_ref, nact_ref,
                    xn_hbm, wg_hbm, wu_hbm, wd_hbm, y_hbm,
                    xbuf, ybuf, wg_buf, wu_buf, wd_buf, gsem, ssem, wsem):
    t = pl.program_id(0)
    nact = nact_ref[0]
    slot = t % 2

    def for_groups(tile, fn):
        n = cnt_ref[tile]
        for g in range(N_ROW_GROUPS):
            pl.when(n > g * ROW_GROUP)(functools.partial(fn, g))

    def start_gather(tile, sl):
        def group(g):
            row0 = row0_ref[tile]
            for r in range(g * ROW_GROUP, (g + 1) * ROW_GROUP):
                tok = item_ref[row0 + r] & TOK_MASK
                pltpu.make_async_copy(xn_hbm.at[pl.ds(tok, 1), :], xbuf.at[sl, pl.ds(r, 1), :], gsem.at[sl]).start()
        for_groups(tile, group)

    def start_scatter(tile, sl):
        def group(g):
            row0, n = row0_ref[tile], cnt_ref[tile]
            for r in range(g * ROW_GROUP, (g + 1) * ROW_GROUP):
                it = jnp.where(r < n, item_ref[row0 + r] >> TOK_BITS, N_ITEMS + sl * ROW_TILE + r)
                pltpu.make_async_copy(ybuf.at[sl, pl.ds(r, 1), :], y_hbm.at[pl.ds(it, 1), :], ssem.at[sl]).start()
        for_groups(tile, group)

    def wait_gather(tile, sl):
        def group(g):
            rows = pl.ds(g * ROW_GROUP, ROW_GROUP)
            pltpu.make_async_copy(xn_hbm.at[rows, :], xbuf.at[sl, rows, :], gsem.at[sl]).wait()
        for_groups(tile, group)

    def wait_scatter(tile, sl):
        def group(g):
            rows = pl.ds(g * ROW_GROUP, ROW_GROUP)
            pltpu.make_async_copy(ybuf.at[sl, rows, :], y_hbm.at[rows, :], ssem.at[sl]).wait()
        for_groups(tile, group)

    def weight_copies(e, par):
        idx = (layer, lax.div(e, EXPERTS_PER_GROUP), lax.rem(e, EXPERTS_PER_GROUP))
        return [pltpu.make_async_copy(hbm.at[idx], buf.at[par], wsem.at[par])
                for hbm, buf in ((wg_hbm, wg_buf), (wu_hbm, wu_buf), (wd_hbm, wd_buf))]

    @pl.when((t == 0) & (nact > 0))
    def _():
        xbuf[...] = jnp.zeros(xbuf.shape, F32)
        start_gather(0, 0)
        for cp in weight_copies(te_ref[0], par_ref[0]):
            cp.start()

    @pl.when(t < nact)
    def _():
        par = par_ref[t]

        @pl.when(first_ref[t] == 1)
        def _():
            for cp in weight_copies(te_ref[t], par):
                cp.wait()

            @pl.when(next_ref[t] >= 0)
            def _():
                for cp in weight_copies(next_ref[t], 1 - par):
                    cp.start()

        def tile_body(sl):
            wait_gather(t, sl)

            @pl.when(t >= 2)
            def _():
                wait_scatter(t - 2, sl)

            start_gather(t + 1, 1 - sl)
            x = xbuf[sl].astype(BF16)
            gate = _dot(x, wg_buf[par].astype(BF16))
            up = _dot(x, wu_buf[par].astype(BF16))
            hid = gate * jax.nn.sigmoid(gate) * up
            ybuf[sl] = _dot(hid.astype(BF16), wd_buf[par].astype(BF16))
            start_scatter(t, sl)

            @pl.when(t == nact - 1)
            def _():
                wait_scatter(t, sl)

                @pl.when(t >= 1)
                def _():
                    wait_scatter(t - 1, 1 - sl)

        for sl in range(2):
            pl.when(slot == sl)(functools.partial(tile_body, sl))


def _experts(layer, plan, xn, w_gate, w_up, w_down):
    any_spec = pl.BlockSpec(memory_space=pl.ANY)
    grid_spec = pltpu.PrefetchScalarGridSpec(
        num_scalar_prefetch=len(plan),
        grid=(MAX_TILES,),
        in_specs=[any_spec, any_spec, any_spec, any_spec],
        out_specs=any_spec,
        scratch_shapes=[
            pltpu.VMEM((2, ROW_TILE, D_MODEL), F32),
            pltpu.VMEM((2, ROW_TILE, D_MODEL), F32),
            pltpu.VMEM((2, D_MODEL, EXPERT_FF), F32),
            pltpu.VMEM((2, D_MODEL, EXPERT_FF), F32),
            pltpu.VMEM((2, EXPERT_FF, D_MODEL), F32),
            pltpu.SemaphoreType.DMA((2,)),
            pltpu.SemaphoreType.DMA((2,)),
            pltpu.SemaphoreType.DMA((2,)),
        ],
    )
    return pl.pallas_call(
        functools.partial(_experts_kernel, layer),
        grid_spec=grid_spec,
        out_shape=jax.ShapeDtypeStruct((N_ITEMS + 2 * ROW_TILE, D_MODEL), F32),
        compiler_params=_cparams(("arbitrary",)),
        name=f"experts{layer}",
    )(*plan, xn, w_gate, w_up, w_down)


def _moe(layer, h, g_ffn, wr_t, br, w_gate, w_up, w_down):
    xn, ids, tw = _router(h, g_ffn, wr_t, br)
    y = _experts(layer, _dispatch_plan(ids), xn, w_gate, w_up, w_down)
    return y, tw.reshape(N_ITEMS, 1)


POST_TM = 320


def _rope_slab(slab, tab):
    z = slab * tab
    return z + pltpu.roll(z, LANES // 2, 1)


def _post_moe0_kernel(h_ref, ya_ref, yb_ref, wa_ref, wb_ref, gkv_ref, wkv_ref, gc_ref, tab_ref, wk_ref, wv_ref,
                      h2_ref, kv_ref, k_ref, v_ref):
    h2 = h_ref[...] + ya_ref[...] * wa_ref[...] + yb_ref[...] * wb_ref[...]
    h2_ref[...] = h2
    a = _dot(_rms(h2, gkv_ref[...]).astype(BF16), wkv_ref[...])
    ckv = _rms(a[:, :KV_LORA_RANK], gc_ref[...])
    pe = _rope_slab(a[:, KV_LORA_RANK:], tab_ref[...])
    kv_ref[:, :KV_LORA_RANK] = ckv
    kv_ref[:, KV_LORA_RANK:] = pe[:, :QK_ROPE_DIM]
    lane = lax.broadcasted_iota(jnp.int32, pe.shape, 1)
    pe_pad = jnp.where(lane < QK_ROPE_DIM, pe, 0.0).astype(BF16)
    cb = ckv.astype(BF16)
    kn = _dot(cb, wk_ref[...]).astype(BF16)
    v_ref[...] = _dot(cb, wv_ref[...]).astype(BF16)
    for hd in range(N_HEADS):
        k_ref[:, hd * HEAD_PAD:hd * HEAD_PAD + LANES] = kn[:, hd * LANES:(hd + 1) * LANES]
        k_ref[:, hd * HEAD_PAD + LANES:(hd + 1) * HEAD_PAD] = pe_pad


def _post_moe0(h, y, w, g_kv, w_kv_ext, g_ckv, tab, wk, wv):
    nb = N_TOK // POST_TM
    row = lambda w: pl.BlockSpec((POST_TM, w), lambda i: (i, 0))
    row2 = lambda w: pl.BlockSpec((POST_TM, w), lambda i: (i + nb, 0))
    full = lambda a: pl.BlockSpec(a.shape, lambda i: (0, 0))
    return pl.pallas_call(
        _post_moe0_kernel,
        grid=(nb,),
        in_specs=[row(D_MODEL), row(D_MODEL), row2(D_MODEL), row(1), row2(1),
                  full(g_kv), full(w_kv_ext), full(g_ckv), row(LANES), full(wk), full(wv)],
        out_specs=[row(D_MODEL), row(LATENT_DIM), row(N_HEADS * HEAD_PAD), row(N_HEADS * V_HEAD_DIM)],
        out_shape=[
            jax.ShapeDtypeStruct((N_TOK, D_MODEL), F32),
            jax.ShapeDtypeStruct((N_TOK, LATENT_DIM), F32),
            jax.ShapeDtypeStruct((N_TOK, N_HEADS * HEAD_PAD), BF16),
            jax.ShapeDtypeStruct((N_TOK, N_HEADS * V_HEAD_DIM), BF16),
        ],
        compiler_params=_cparams(("parallel",)),
        name="post_moe0",
    )(h, y, y, w, w, g_kv, w_kv_ext, g_ckv, tab, wk, wv)


def _q_proj_kernel(h_ref, g_ref, wdq_ref, gq_ref, wuq_ref, tab_ref, q_ref):
    xn = _rms(h_ref[...], g_ref[...]).astype(BF16)
    cq = _rms(_dot(xn, wdq_ref[...]), gq_ref[...]).astype(BF16)
    q = _dot(cq, wuq_ref[...]) * (SOFTMAX_SCALE * LOG2E)
    tab = tab_ref[...]
    for hd in range(N_HEADS):
        lo = hd * HEAD_PAD
        q_ref[:, lo:lo + LANES] = q[:, lo:lo + LANES].astype(BF16)
        q_ref[:, lo + LANES:lo + HEAD_PAD] = _rope_slab(q[:, lo + LANES:lo + HEAD_PAD], tab).astype(BF16)


def _q_proj(h, g, wdq, gq, wuq_ext, tab):
    row = lambda w: pl.BlockSpec((POST_TM, w), lambda i: (i, 0))
    full = lambda a: pl.BlockSpec(a.shape, lambda i: (0, 0))
    return pl.pallas_call(
        _q_proj_kernel,
        grid=(N_TOK // POST_TM,),
        in_specs=[row(D_MODEL), full(g), full(wdq), full(gq), full(wuq_ext), row(LANES)],
        out_specs=row(N_HEADS * HEAD_PAD),
        out_shape=jax.ShapeDtypeStruct((N_TOK, N_HEADS * HEAD_PAD), BF16),
        compiler_params=_cparams(("parallel",)),
        name="q_proj",
    )(h, g, wdq, gq, wuq_ext, tab)


ATT_T = 1024
ATT_NB = N_PROMPT // ATT_T
ATT_RC = 256
_PAIRS = [(qi, kj) for qi in range(ATT_NB) for kj in range(qi + 1)]
_PAIR_Q = np.array([p[0] for p in _PAIRS], np.int32)
_PAIR_K = np.array([p[1] for p in _PAIRS], np.int32)


def _attn_tile(q_ref, k_ref, v_ref, m_ref, acc_ref, diagonal):
    k = k_ref[...]
    v_ext = jnp.concatenate([v_ref[...], jnp.ones((ATT_T, LANES), BF16)], axis=1)
    for r0 in range(0, ATT_T, ATT_RC):
        rs = slice(r0, r0 + ATT_RC)
        s = _dot_nt(q_ref[rs, :], k)
        if diagonal:
            rows = r0 + lax.broadcasted_iota(jnp.int32, s.shape, 0)
            cols = lax.broadcasted_iota(jnp.int32, s.shape, 1)
            s = jnp.where(cols <= rows, s, -jnp.inf)
        m_old = m_ref[rs, :]
        m_new = jnp.maximum(m_old, jnp.max(s, axis=-1, keepdims=True))
        p = jnp.exp2(s - m_new).astype(BF16)
        acc_ref[rs, :] = acc_ref[rs, :] * jnp.exp2(m_old - m_new) + _dot(p, v_ext)
        m_ref[rs, :] = m_new


def _attn_prompt_kernel(pq_ref, pk_ref, q_ref, k_ref, v_ref, o_ref, m_ref, acc_ref):
    p = pl.program_id(1)
    qi = pq_ref[p]
    kj = pk_ref[p]

    @pl.when(kj == 0)
    def _():
        m_ref[...] = jnp.full(m_ref.shape, -jnp.inf, F32)
        acc_ref[...] = jnp.zeros(acc_ref.shape, F32)

    @pl.when(kj < qi)
    def _():
        _attn_tile(q_ref, k_ref, v_ref, m_ref, acc_ref, diagonal=False)

    @pl.when(kj == qi)
    def _():
        _attn_tile(q_ref, k_ref, v_ref, m_ref, acc_ref, diagonal=True)
        acc = acc_ref[...]
        o_ref[...] = (acc[:, :V_HEAD_DIM] / acc[:, V_HEAD_DIM:]).astype(BF16)


def _attn_prompt(q, k, v):
    grid_spec = pltpu.PrefetchScalarGridSpec(
        num_scalar_prefetch=2,
        grid=(N_HEADS, len(_PAIRS)),
        in_specs=[
            pl.BlockSpec((ATT_T, HEAD_PAD), lambda h, p, pq, pk: (pq[p], h)),
            pl.BlockSpec((ATT_T, HEAD_PAD), lambda h, p, pq, pk: (pk[p], h)),
            pl.BlockSpec((ATT_T, V_HEAD_DIM), lambda h, p, pq, pk: (pk[p], h)),
        ],
        out_specs=pl.BlockSpec((ATT_T, V_HEAD_DIM), lambda h, p, pq, pk: (pq[p], h)),
        scratch_shapes=[
            pltpu.VMEM((ATT_T, 1), F32),
            pltpu.VMEM((ATT_T, V_HEAD_DIM + LANES), F32),
        ],
    )
    return pl.pallas_call(
        _attn_prompt_kernel,
        grid_spec=grid_spec,
        out_shape=jax.ShapeDtypeStruct((N_TOK, N_HEADS * V_HEAD_DIM), BF16),
        compiler_params=_cparams(("parallel", "arbitrary")),
        name="attn_prompt",
    )(jnp.asarray(_PAIR_Q), jnp.asarray(_PAIR_K), q, k, v)


def _q_lat_kernel(q_ref, wuk_ref, o_ref):
    q = q_ref[...]
    o_ref[:, :KV_LORA_RANK] = _dot(q[:, :QK_NOPE_DIM], wuk_ref[...]).astype(BF16)
    o_ref[:, KV_LORA_RANK:] = q[:, QK_NOPE_DIM:]


def _q_lat(q, w_uk_t):
    return pl.pallas_call(
        _q_lat_kernel,
        grid=(N_HEADS,),
        in_specs=[
            pl.BlockSpec((N_SAMPLE, HEAD_PAD), lambda h: (N_PROMPT // N_SAMPLE, h)),
            pl.BlockSpec((None, QK_NOPE_DIM, KV_LORA_RANK), lambda h: (h, 0, 0)),
        ],
        out_specs=pl.BlockSpec((None, N_SAMPLE, KV_EXT), lambda h: (h, 0, 0)),
        out_shape=jax.ShapeDtypeStruct((N_HEADS, N_SAMPLE, KV_EXT), BF16),
        compiler_params=_cparams(("parallel",)),
        name="q_lat",
    )(q, w_uk_t)


DEC_PAGES = 16
DEC_BB = 2
DEC_CHUNKS = N_PAGES // DEC_PAGES
DEC_STEPS = (N_SAMPLE // DEC_BB) * DEC_CHUNKS


def _attn_sample_kernel(pt_ref, q_ref, kvn_ref, cache_hbm, o_ref, kvbuf, sems, m_ref, l_ref, acc_ref):
    c = pl.program_id(1)
    step = pl.program_id(0) * DEC_CHUNKS + c
    slot = step % 2

    def page_copy(st, sl, j, i):
        seq = lax.div(st, DEC_CHUNKS) * DEC_BB + j
        pg = pt_ref[seq * N_PAGES + lax.rem(st, DEC_CHUNKS) * DEC_PAGES + i]
        return pltpu.make_async_copy(cache_hbm.at[pg], kvbuf.at[sl, j, i], sems.at[sl])

    def start_all(st, sl):
        for j in range(DEC_BB):
            for i in range(DEC_PAGES):
                page_copy(st, sl, j, i).start()

    @pl.when(step == 0)
    def _():
        start_all(0, 0)

    @pl.when(step + 1 < DEC_STEPS)
    def _():
        start_all(step + 1, 1 - slot)

    for j in range(DEC_BB):
        for i in range(DEC_PAGES):
            page_copy(step, slot, j, i).wait()

    for j in range(DEC_BB):
        q = q_ref[j][:, :LATENT_DIM]

        @pl.when(c == 0)
        def _():
            kvn = kvn_ref[j].astype(BF16).astype(F32)
            m_ref[j] = jnp.sum(q.astype(F32) * kvn, axis=-1, keepdims=True)
            l_ref[j] = jnp.ones((N_HEADS, 1), F32)
            acc_ref[j] = jnp.broadcast_to(kvn[:, :KV_LORA_RANK], (N_HEADS, KV_LORA_RANK))

    for j in range(DEC_BB):
        q = q_ref[j][:, :LATENT_DIM]
        kv_t = jnp.concatenate([kvbuf[slot, j, i].astype(BF16) for i in range(DEC_PAGES)], axis=1)
        s = _dot(q, kv_t)
        m_old = m_ref[j]
        m_new = jnp.maximum(m_old, jnp.max(s, axis=-1, keepdims=True))
        corr = jnp.exp2(m_old - m_new)
        e = jnp.exp2(s - m_new)
        l_ref[j] = l_ref[j] * corr + jnp.sum(e, axis=-1, keepdims=True)
        acc_ref[j] = acc_ref[j] * corr + _dot_nt(e.astype(BF16), kv_t[:KV_LORA_RANK])
        m_ref[j] = m_new

    @pl.when(c == DEC_CHUNKS - 1)
    def _():
        o_ref[...] = acc_ref[...] / l_ref[...]


def _attn_sample(page_table, q_lat, kv_new, cache_t):
    grid_spec = pltpu.PrefetchScalarGridSpec(
        num_scalar_prefetch=1,
        grid=(N_SAMPLE // DEC_BB, DEC_CHUNKS),
        in_specs=[
            pl.BlockSpec((DEC_BB, N_HEADS, KV_EXT), lambda b, c, pt: (b, 0, 0)),
            pl.BlockSpec((DEC_BB, 1, LATENT_DIM), lambda b, c, pt: (b, 0, 0)),
            pl.BlockSpec(memory_space=pl.ANY),
        ],
        out_specs=pl.BlockSpec((DEC_BB, N_HEADS, KV_LORA_RANK), lambda b, c, pt: (b, 0, 0)),
        scratch_shapes=[
            pltpu.VMEM((2, DEC_BB, DEC_PAGES, LATENT_DIM, PAGE_SIZE), F32),
            pltpu.SemaphoreType.DMA((2,)),
            pltpu.VMEM((DEC_BB, N_HEADS, 1), F32),
            pltpu.VMEM((DEC_BB, N_HEADS, 1), F32),
            pltpu.VMEM((DEC_BB, N_HEADS, KV_LORA_RANK), F32),
        ],
    )
    return pl.pallas_call(
        _attn_sample_kernel,
        grid_spec=grid_spec,
        out_shape=jax.ShapeDtypeStruct((N_SAMPLE, N_HEADS, KV_LORA_RANK), F32),
        compiler_params=_cparams(("arbitrary", "arbitrary")),
        name="attn_sample",
    )(page_table.reshape(-1), q_lat, kv_new, cache_t)


def _uv_sample_kernel(o_any, lat_ref, wuv_ref, o_ref):
    del o_any
    o_ref[...] = _dot(lat_ref[...].astype(BF16), wuv_ref[...]).astype(BF16)


def _uv_sample(o_buf, lat_t, w_uv):
    return pl.pallas_call(
        _uv_sample_kernel,
        grid=(N_HEADS,),
        in_specs=[
            pl.BlockSpec(memory_space=pl.ANY),
            pl.BlockSpec((None, N_SAMPLE, KV_LORA_RANK), lambda h: (h, 0, 0)),
            pl.BlockSpec((None, KV_LORA_RANK, V_HEAD_DIM), lambda h: (h, 0, 0)),
        ],
        out_specs=pl.BlockSpec((N_SAMPLE, V_HEAD_DIM), lambda h: (N_PROMPT // N_SAMPLE, h)),
        out_shape=jax.ShapeDtypeStruct((N_TOK, N_HEADS * V_HEAD_DIM), BF16),
        input_output_aliases={0: 0},
        compiler_params=_cparams(("arbitrary",)),
        name="uv_sample",
    )(o_buf, lat_t, w_uv)


def _out_proj_kernel(o_ref, h_ref, w_ref, h3_ref):
    h3_ref[...] = h_ref[...] + _dot(o_ref[...], w_ref[...])


def _out_proj(o, h, w_o):
    row = pl.BlockSpec((POST_TM, D_MODEL), lambda i: (i, 0))
    return pl.pallas_call(
        _out_proj_kernel,
        grid=(N_TOK // POST_TM,),
        in_specs=[row, row, pl.BlockSpec((D_MODEL, D_MODEL), lambda i: (0, 0))],
        out_specs=row,
        out_shape=jax.ShapeDtypeStruct((N_TOK, D_MODEL), F32),
        compiler_params=_cparams(("parallel",)),
        name="out_proj",
    )(o, h, w_o)


def _final_kernel(h_ref, ya_ref, yb_ref, wa_ref, wb_ref, g_ref, o_ref):
    o_ref[...] = _rms(h_ref[...] + ya_ref[...] * wa_ref[...] + yb_ref[...] * wb_ref[...], g_ref[...])


def _final(h, y, w, g):
    nb = N_TOK // POST_TM
    row = lambda w: pl.BlockSpec((POST_TM, w), lambda i: (i, 0))
    row2 = lambda w: pl.BlockSpec((POST_TM, w), lambda i: (i + nb, 0))
    return pl.pallas_call(
        _final_kernel,
        grid=(nb,),
        in_specs=[row(D_MODEL), row(D_MODEL), row2(D_MODEL), row(1), row2(1),
                  pl.BlockSpec((1, D_MODEL), lambda i: (0, 0))],
        out_specs=row(D_MODEL),
        out_shape=jax.ShapeDtypeStruct((N_TOK, D_MODEL), F32),
        compiler_params=_cparams(("parallel",)),
        name="final",
    )(h, y, y, w, w, g)


def _rope_table():
    half = QK_ROPE_DIM // 2
    inv = (np.float32(ROPE_THETA) ** (-np.arange(half, dtype=np.float32) / np.float32(half))).astype(np.float32)
    pos = np.concatenate([np.arange(N_PROMPT), np.full((N_SAMPLE,), PAST_LEN)]).astype(np.float32)
    ang = (pos[:, None] * inv[None, :]).astype(np.float32)
    c, s = np.cos(ang.astype(np.float64)), np.sin(ang.astype(np.float64))
    return jnp.asarray(np.concatenate([c, c, -s, s], axis=1).astype(np.float32))


def _rot_half_cols(w):
    half = QK_ROPE_DIM // 2
    return jnp.concatenate([w[..., half:], w[..., :half]], axis=-1)


def kernel(x_prompt, x_sample, state_conv, cache_kv_latent, page_table, g_mix, g_ffn, g_final, w_pw1, b_pw1, w_dw,
           b_dw, g_conv_ln, b_conv_ln, w_pw2, b_pw2, g_kv_in, w_dkv, g_ckv, w_uk, w_uv, w_dq, g_q_a, w_uq, w_o,
           w_router_group, b_router_group, w_router_expert, b_router_expert, w_exp_gate, w_exp_up, w_exp_down):
    vec = lambda a: a.reshape(1, -1)
    x = jnp.concatenate([x_prompt[0], x_sample[:, 0]], axis=0)
    tab = _rope_table()

    w1 = w_pw1[0].astype(BF16)
    w2 = w_pw2[0].astype(BF16)
    w_kv_ext = jnp.concatenate([w_dkv, _rot_half_cols(w_dkv[:, KV_LORA_RANK:])], axis=1).astype(BF16)
    wk = w_uk.transpose(1, 0, 2).reshape(KV_LORA_RANK, N_HEADS * QK_NOPE_DIM).astype(BF16)
    wv = w_uv.transpose(1, 0, 2).reshape(KV_LORA_RANK, N_HEADS * V_HEAD_DIM).astype(BF16)
    w_uk_t = w_uk.transpose(0, 2, 1).astype(BF16)
    w_uv_b = w_uv.astype(BF16)
    wdq = w_dq[0].astype(BF16)
    wuq = w_uq[0]
    wuq_ext = jnp.concatenate([wuq, _rot_half_cols(wuq[..., QK_NOPE_DIM:])], axis=-1)
    wuq_ext = wuq_ext.reshape(Q_LORA_RANK, N_HEADS * HEAD_PAD).astype(BF16)
    wo = w_o[0].astype(BF16)
    wr_t = [jnp.concatenate([w_router_group[l], w_router_expert[l]], axis=1).T for l in range(2)]
    br = [jnp.concatenate([b_router_group[l], b_router_expert[l]]).reshape(-1, 1) for l in range(2)]

    glu = _pw1_glu(x, vec(g_mix[0]), w1, vec(b_pw1[0]))
    conv_args = (w_dw[0], vec(b_dw[0]), vec(g_conv_ln[0]), vec(b_conv_ln[0]), w2, vec(b_pw2[0]))
    h1 = _conv_prompt(glu, x, *conv_args)
    h1, state_new_t = _conv_sample(h1, state_conv[0].transpose(1, 0, 2), glu, x, *conv_args)

    y0, cw0 = _moe(0, h1, vec(g_ffn[0]), wr_t[0], br[0], w_exp_gate, w_exp_up, w_exp_down)
    h2, kv_rows, k_heads, v_heads = _post_moe0(h1, y0, cw0, vec(g_kv_in), w_kv_ext, vec(g_ckv), tab, wk, wv)

    q = _q_proj(h2, vec(g_mix[1]), wdq, vec(g_q_a[0]), wuq_ext, tab)
    o = _attn_prompt(q, k_heads, v_heads)
    q_lat = _q_lat(q, w_uk_t).transpose(1, 0, 2)
    lat = _attn_sample(page_table, q_lat, kv_rows[N_PROMPT:].reshape(N_SAMPLE, 1, LATENT_DIM),
                       cache_kv_latent.transpose(0, 2, 1))
    o = _uv_sample(o, lat.transpose(1, 0, 2), w_uv_b)
    h3 = _out_proj(o, h2, wo)

    y1, cw1 = _moe(1, h3, vec(g_ffn[1]), wr_t[1], br[1], w_exp_gate, w_exp_up, w_exp_down)
    out = _final(h3, y1, cw1, vec(g_final))

    conv_state_prompt = glu[N_PROMPT - (CONV_WIDTH - 1):N_PROMPT][None, None]
    conv_state_sample = state_new_t.transpose(1, 0, 2)[None]
    return (out[:N_PROMPT][None], out[N_PROMPT:][:, None], conv_state_prompt, conv_state_sample,
            kv_rows[:N_PROMPT][None], kv_rows[N_PROMPT:][:, None])
```
